```python
import jax, jax.numpy as jnp
from jax import lax
import numpy as np

D_MODEL = 1024
BATCH = 8
SEQ = 8192
DEPTH = 2
DEC_BATCH = 32
DEC_SEQ = 2048
PAST_LEN = 128

PLE_DIM = 256
CHUNK = 128
A_WIDTH = 512
A_GROUPS = 4
B_WIDTH = 512
CONV_WIDTH = 3
C_WIDTH = 512
C_GROUPS = 4
C_GROUP_WIDTH = C_WIDTH // C_GROUPS
POOL_WINDOWS = (2, 4, 8, 16)
ATT_CONFIGS = ((128, 1), (512, 4), (2048, 16))
ATT_HEADS = 4
HEAD_DIM = 128
D_WIDTH = ATT_HEADS * HEAD_DIM
N_BRANCH = 4
BRANCH_WIDTH = 512
N_IN = 2 * A_WIDTH + 3 * B_WIDTH + C_WIDTH + 3 * len(ATT_CONFIGS) * D_WIDTH
N_EXPERTS = 32
TOP_K = 4
D_FF = 1024
SWIGLU_LIMIT = 7.0
SWIGLU_ALPHA = 1.702
EXPERT_BLOCK = 256
ALPHA = (2 * DEPTH) ** 0.25
BETA = (8 * DEPTH) ** -0.25
LN_EPS = 1e-5
NEG_INF = -1e30

kernel_name = 'hybrid_bidir_encoder_two_groups'


def layer_norm(x, gain=None, bias=None):
    xf = x.astype(jnp.float32)
    xc = xf - jnp.mean(xf, axis=-1, keepdims=True)
    y = xc * lax.rsqrt(jnp.mean(xc * xc, axis=-1, keepdims=True) + LN_EPS)
    if gain is not None:
        y = y * gain.astype(jnp.float32) + bias.astype(jnp.float32)
    return y.astype(x.dtype)


def split_points():
    sizes = [A_WIDTH, A_WIDTH, B_WIDTH, B_WIDTH, B_WIDTH, C_WIDTH] + [D_WIDTH] * (3 * len(ATT_CONFIGS))
    return [int(o) for o in np.cumsum(sizes)[:-1]]


def alibi_slopes():
    n = len(ATT_CONFIGS) * ATT_HEADS
    slopes = 2.0 ** (-8.0 * np.arange(1, n + 1) / n)
    return jnp.asarray(slopes, dtype=jnp.float32).reshape(len(ATT_CONFIGS), ATT_HEADS)


def spatial_gating(u, v, w_s, b_s):
    bn, s, _ = v.shape
    vn = layer_norm(v).reshape(bn, s // CHUNK, CHUNK, A_GROUPS, A_WIDTH // A_GROUPS)
    mixed = jnp.einsum('gts,bnsgc->bntgc', w_s, vn) + b_s.T[None, None, :, :, None]
    return u * mixed.reshape(bn, s, A_WIDTH)


def gated_short_conv(h, gate_b, gate_c, conv_w):
    z = gate_c * h
    zp = jnp.pad(z, ((0, 0), (1, 1), (0, 0)))
    conv = conv_w[0] * zp[:, :-2] + conv_w[1] * zp[:, 1:-1] + conv_w[2] * zp[:, 2:]
    return gate_b * conv


def multiscale_pool(z, pool_w, pool_scale):
    bn, s, _ = z.shape
    zf = z.astype(jnp.float32)
    cs = jnp.pad(jnp.cumsum(zf, axis=1), ((0, 0), (1, 0), (0, 0)))
    csg = cs.reshape(bn, s + 1, C_GROUPS, C_GROUP_WIDTH)
    zg = zf.reshape(bn, s, C_GROUPS, C_GROUP_WIDTH)
    t = np.arange(s)
    outs = []
    for g, w in enumerate(POOL_WINDOWS):
        lo = np.clip(t - w // 2, 0, s)
        hi = np.clip(t + w // 2, 0, s)
        cs_g = csg[:, :, g]
        window_sum = jnp.take(cs_g, hi, axis=1) - jnp.take(cs_g, lo, axis=1)
        count = jnp.asarray((hi - lo), dtype=jnp.float32)[None, :, None]
        outs.append(window_sum / count - zg[:, :, g])
    pooled = jnp.stack(outs, axis=2).astype(z.dtype)
    y = jnp.einsum('bsgc,gcd->bsgd', pooled, pool_w)
    return y.reshape(bn, s, C_WIDTH) * pool_scale


def dilated_window_attention(q, k, v, slopes, dilation, radius):
    bn, s, nh, hd = q.shape
    L = s // dilation
    n = bn * dilation
    nb = -(-L // radius)
    Lp = nb * radius

    def to_sub(t):
        return t.reshape(bn, L, dilation, nh, hd).transpose(0, 2, 1, 3, 4).reshape(n, L, nh, hd)

    def band(t):
        tb = jnp.pad(t, ((0, 0), (radius, Lp - L + radius), (0, 0), (0, 0))).reshape(n, nb + 2, radius, nh, hd)
        return jnp.concatenate([tb[:, :-2], tb[:, 1:-1], tb[:, 2:]], axis=2)

    qs, ks, vs = to_sub(q), to_sub(k), to_sub(v)
    qb = jnp.pad(qs, ((0, 0), (0, Lp - L), (0, 0), (0, 0))).reshape(n, nb, radius, nh, hd)
    kw, vw = band(ks), band(vs)
    rel = np.arange(3 * radius)[None, :] - radius - np.arange(radius)[:, None]
    kabs = np.arange(nb)[:, None] * radius + np.arange(3 * radius)[None, :] - radius
    valid = (np.abs(rel)[None] <= radius) & ((kabs >= 0) & (kabs < L))[:, None, :]
    dist = jnp.asarray(np.abs(rel) * dilation, dtype=jnp.float32)
    bias = -slopes.astype(jnp.float32)[:, None, None] * dist[None]
    scores = jnp.einsum('nbqhd,nbkhd->nbhqk', qb, kw, preferred_element_type=jnp.float32) * (hd ** -0.5)
    scores = jnp.where(jnp.asarray(valid)[None, :, None], scores + bias[None, None], NEG_INF)
    lse = jax.nn.logsumexp(scores, axis=-1)
    prob = jnp.exp(scores - lse[..., None]).astype(v.dtype)
    o = jnp.einsum('nbhqk,nbkhd->nbqhd', prob, vw)

    def from_sub(t):
        t = t.reshape((n, Lp) + t.shape[3:])[:, :L]
        return t.reshape((bn, dilation, L) + t.shape[2:]).swapaxes(1, 2).reshape((bn, s) + t.shape[2:])

    return from_sub(o), from_sub(lse.transpose(0, 1, 3, 2))


def dilated_mixture(qkv):
    slopes = alibi_slopes()
    outs, lses = [], []
    for g, (window, dil) in enumerate(ATT_CONFIGS):
        q, k, v = [t.reshape(t.shape[0], t.shape[1], ATT_HEADS, HEAD_DIM) for t in qkv[3 * g:3 * g + 3]]
        o, l = dilated_window_attention(q, k, v, slopes[g], dil, window // (2 * dil))
        outs.append(o)
        lses.append(l)
    wts = jax.nn.softmax(jnp.stack(lses), axis=0).astype(outs[0].dtype)
    y = jnp.einsum('gbsh,gbshd->bshd', wts, jnp.stack(outs))
    return y.reshape(y.shape[0], y.shape[1], D_WIDTH)


def routed_experts(x2, router_w, router_b, w_gate_up, b_gate_up, w_down, b_down):
    T, _ = x2.shape
    A = T * TOP_K
    M = EXPERT_BLOCK
    NB = -(-(A + N_EXPERTS * (M - 1)) // M)
    logits = jnp.dot(x2, router_w, preferred_element_type=jnp.float32) + router_b.astype(jnp.float32)
    top_val, top_idx = lax.top_k(logits, TOP_K)
    probs = jax.nn.softmax(top_val, axis=-1)
    e_flat = top_idx.reshape(A)
    order = jnp.argsort(e_flat)
    e_sorted = e_flat[order]
    counts = jnp.bincount(e_flat, length=N_EXPERTS)
    padded = (counts + M - 1) // M * M
    start_sorted = jnp.cumsum(counts) - counts
    ends_padded = jnp.cumsum(padded)
    start_padded = ends_padded - padded
    dest = start_padded[e_sorted] + jnp.arange(A) - start_sorted[e_sorted]
    slot_tok = jnp.zeros((NB * M,), jnp.int32).at[dest].set((order // TOP_K).astype(jnp.int32))
    slot_w = jnp.zeros((NB * M,), jnp.float32).at[dest].set(probs.reshape(A)[order])
    block_e = jnp.minimum(jnp.searchsorted(ends_padded, jnp.arange(NB) * M, side='right'), N_EXPERTS - 1)

    def expert_block(args):
        tok, wt, e = args
        h = jnp.dot(x2[tok], w_gate_up[e]) + b_gate_up[e]
        g = jnp.minimum(h[:, :D_FF], SWIGLU_LIMIT)
        u = jnp.clip(h[:, D_FF:], -SWIGLU_LIMIT, SWIGLU_LIMIT)
        act = g * jax.nn.sigmoid(SWIGLU_ALPHA * g) * (u + 1.0)
        return (jnp.dot(act, w_down[e]) + b_down[e]) * wt[:, None].astype(x2.dtype)

    y = lax.map(expert_block, (slot_tok.reshape(NB, M), slot_w.reshape(NB, M), block_e))
    return jax.ops.segment_sum(y.reshape(NB * M, -1), slot_tok, num_segments=T)


def encoder_layer(x, p, w_in, spatial_w, spatial_b, conv_w, pool_w, pool_scale, w_branch, w_gate, w_out,
                  ln1_g, ln1_b, router_w, router_b, w_gate_up, b_gate_up, w_down, b_down,
                  w_ple_gate, w_ple_proj, ln2_g, ln2_b):
    bn, s, d = x.shape
    h = x @ w_in
    parts = jnp.split(h, split_points(), axis=-1)
    a_u, a_v, b_h, b_b, b_c, c_z = parts[:6]
    branches = (
        spatial_gating(a_u, a_v, spatial_w, spatial_b),
        gated_short_conv(b_h, b_b, b_c, conv_w),
        multiscale_pool(c_z, pool_w, pool_scale),
        dilated_mixture(parts[6:]),
    )
    merged = None
    for i, y in enumerate(branches):
        term = jax.nn.sigmoid(x @ w_gate[i]) * (y @ w_branch[i])
        merged = term if merged is None else merged + term
    x = layer_norm(ALPHA * x + merged @ w_out, ln1_g, ln1_b)
    moe = routed_experts(x.reshape(bn * s, d), router_w, router_b, w_gate_up, b_gate_up,
                         w_down, b_down).reshape(bn, s, d)
    ple = jax.nn.sigmoid(x @ w_ple_gate) * (p @ w_ple_proj)
    return layer_norm(ALPHA * x + moe + ple, ln2_g, ln2_b)


def run_trunk(x, p, weights):
    for i in range(DEPTH):
        x = encoder_layer(x, p[i], *[w[i] for w in weights])
    return x


def setup_inputs(seed: int = 0) -> dict:
    key = jax.random.key(seed)
    ks = jax.random.split(key, 26)
    f32 = jnp.float32

    def nrm(k, shape, scale):
        return jax.random.normal(k, shape, f32) * scale

    D, E, F = D_MODEL, N_EXPERTS, D_FF
    return {
        'x_prompt': nrm(ks[0], (BATCH, SEQ, D), 1.0),
        'x_sample': nrm(ks[1], (DEC_BATCH, DEC_SEQ, D), 1.0),
        'p_prompt': nrm(ks[2], (DEPTH, BATCH, SEQ, PLE_DIM), 1.0),
        'p_sample': nrm(ks[3], (DEPTH, DEC_BATCH, DEC_SEQ, PLE_DIM), 1.0),
        'w_in': nrm(ks[4], (DEPTH, D, N_IN), D ** -0.5),
        'spatial_w': nrm(ks[5], (DEPTH, A_GROUPS, CHUNK, CHUNK), CHUNK ** -0.5),
        'spatial_b': 1.0 + nrm(ks[6], (DEPTH, A_GROUPS, CHUNK), 0.02),
        'conv_w': nrm(ks[7], (DEPTH, CONV_WIDTH, B_WIDTH), CONV_WIDTH ** -0.5),
        'pool_w': nrm(ks[8], (DEPTH, C_GROUPS, C_GROUP_WIDTH, C_GROUP_WIDTH), C_GROUP_WIDTH ** -0.5),
        'pool_scale': 1.0 + nrm(ks[9], (DEPTH, C_WIDTH), 0.02),
        'w_branch': nrm(ks[10], (DEPTH, N_BRANCH, BRANCH_WIDTH, D), BRANCH_WIDTH ** -0.5),
        'w_gate': nrm(ks[11], (DEPTH, N_BRANCH, D, D), D ** -0.5),
        'w_out': nrm(ks[12], (DEPTH, D, D), BETA * D ** -0.5),
        'ln1_g': 1.0 + nrm(ks[13], (DEPTH, D), 0.02),
        'ln1_b': nrm(ks[14], (DEPTH, D), 0.02),
        'router_w': nrm(ks[15], (DEPTH, D, E), D ** -0.5),
        'router_b': nrm(ks[16], (DEPTH, E), 0.01),
        'w_gate_up': nrm(ks[17], (DEPTH, E, D, 2 * F), D ** -0.5),
        'b_gate_up': nrm(ks[18], (DEPTH, E, 2 * F), 0.02),
        'w_down': nrm(ks[19], (DEPTH, E, F, D), BETA * F ** -0.5),
        'b_down': nrm(ks[20], (DEPTH, E, D), 0.02),
        'w_ple_gate': nrm(ks[21], (DEPTH, D, D), D ** -0.5),
        'w_ple_proj': nrm(ks[22], (DEPTH, PLE_DIM, D), BETA * PLE_DIM ** -0.5),
        'ln2_g': 1.0 + nrm(ks[23], (DEPTH, D), 0.02),
        'ln2_b': nrm(ks[24], (DEPTH, D), 0.02),
    }


def reference(x_prompt, x_sample, p_prompt, p_sample, w_in, spatial_w, spatial_b, conv_w, pool_w,
              pool_scale, w_branch, w_gate, w_out, ln1_g, ln1_b, router_w, router_b, w_gate_up,
              b_gate_up, w_down, b_down, w_ple_gate, w_ple_proj, ln2_g, ln2_b):
    weights = (w_in, spatial_w, spatial_b, conv_w, pool_w, pool_scale, w_branch, w_gate, w_out,
               ln1_g, ln1_b, router_w, router_b, w_gate_up, b_gate_up, w_down, b_down,
               w_ple_gate, w_ple_proj, ln2_g, ln2_b)
    y_prompt = run_trunk(x_prompt, p_prompt, weights)
    y_sample = run_trunk(x_sample, p_sample, weights)
    return (y_prompt, y_sample)
```

```python
import functools

import numpy as np
import jax
import jax.numpy as jnp
from jax import lax
from jax.experimental import pallas as pl
from jax.experimental.pallas import tpu as pltpu

D_MODEL = 1024
PLE_DIM = 256
CHUNK = 128
A_WIDTH = 512
A_GROUPS = 4
B_WIDTH = 512
C_WIDTH = 512
POOL_WINDOWS = (2, 4, 8, 16)
ATT_CONFIGS = ((128, 1), (512, 4), (2048, 16))
ATT_HEADS = 4
HEAD_DIM = 128
D_WIDTH = ATT_HEADS * HEAD_DIM
N_ABC = 2 * A_WIDTH + 3 * B_WIDTH + C_WIDTH
N_QKV = 3 * D_WIDTH
N_EXPERTS = 32
TOP_K = 4
D_FF = 1024
SWIGLU_LIMIT = 7.0
SWIGLU_ALPHA = 1.702
DEPTH = 2
ALPHA = (2 * DEPTH) ** 0.25
LN_EPS = 1e-5
NEG_INF = -1e30

LANES = 128
HALO = 8
RADIUS = 64
Q_BLOCK = 128
PROJ_ROWS = 1024
ATT_ROWS = 512
MIX_ROWS = 512
EXPERT_ROWS = 256
COMBINE_ROWS = 256
VMEM_LIMIT = 56 * 1024 * 1024

F32 = jnp.float32
BF16 = jnp.bfloat16


def _params(n_axes, vmem=VMEM_LIMIT):
    return pltpu.CompilerParams(dimension_semantics=("arbitrary",) * n_axes, vmem_limit_bytes=vmem)


def _resident(shape):
    zeros = (0,) * len(shape)
    return pl.BlockSpec(shape, lambda *_: zeros, pipeline_mode=pl.Buffered(1))


def _dot(a, b):
    return jnp.dot(a, b, preferred_element_type=F32)


def _layer_norm(x):
    mu = jnp.mean(x, axis=-1, keepdims=True)
    xc = x - mu
    var = jnp.mean(xc * xc, axis=-1, keepdims=True)
    return xc * lax.rsqrt(var + LN_EPS)


def _qkv_kernel(x_ref, w_ref, o_ref):
    bb, rows, d = x_ref.shape
    x = x_ref[...].reshape(bb * rows, d).astype(BF16)
    y = _dot(x, w_ref[...])
    o_ref[...] = y.astype(BF16).reshape(bb, rows, N_QKV)


def _qkv_proj(x, w, dil):
    bn, s, d = x.shape
    sub = s // dil
    rows = min(sub, PROJ_ROWS)
    bb = min(bn, PROJ_ROWS // rows)
    assert sub % rows == 0 and bn % bb == 0
    xv = x.reshape(bn, sub, dil * d)
    return pl.pallas_call(
        _qkv_kernel,
        grid=(bn // bb, dil, sub // rows),
        in_specs=[pl.BlockSpec((bb, rows, d), lambda b, r, i: (b, i, r)),
                  _resident((d, N_QKV))],
        out_specs=pl.BlockSpec((bb, None, rows, N_QKV), lambda b, r, i: (b, r, i, 0)),
        out_shape=jax.ShapeDtypeStruct((bn, dil, sub, N_QKV), BF16),
        compiler_params=_params(3),
        name=f"qkv_proj_d{dil}",
    )(xv, w)


def _attn_kernel(main_ref, kp_ref, vp_ref, kn_ref, vn_ref, o_ref, lse_ref, kbuf, vbuf, *, sub, rows, dil, slopes):
    i = pl.program_id(2)
    kbuf[0:RADIUS, :] = kp_ref[...]
    kbuf[RADIUS:RADIUS + rows, :] = main_ref[:, D_WIDTH:2 * D_WIDTH]
    kbuf[RADIUS + rows:, :] = kn_ref[...]
    vbuf[0:RADIUS, :] = vp_ref[...]
    vbuf[RADIUS:RADIUS + rows, :] = main_ref[:, 2 * D_WIDTH:3 * D_WIDTH]
    vbuf[RADIUS + rows:, :] = vn_ref[...]

    window = Q_BLOCK + 2 * RADIUS
    row = lax.broadcasted_iota(jnp.int32, (Q_BLOCK, window), 0)
    col = lax.broadcasted_iota(jnp.int32, (Q_BLOCK, window), 1)
    dist = jnp.abs(col - RADIUS - row)
    in_band = dist <= RADIUS
    dist_f = (dist * dil).astype(F32)
    lane = lax.broadcasted_iota(jnp.int32, (Q_BLOCK, LANES), 1)
    scale = HEAD_DIM ** -0.5
    for j in range(rows // Q_BLOCK):
        kpos = i * rows + j * Q_BLOCK - RADIUS + col
        valid = in_band & (kpos >= 0) & (kpos < sub)
        lse_tile = jnp.zeros((Q_BLOCK, LANES), F32)
        for h in range(ATT_HEADS):
            hs = slice(h * HEAD_DIM, (h + 1) * HEAD_DIM)
            q = main_ref[j * Q_BLOCK:(j + 1) * Q_BLOCK, hs]
            k = kbuf[j * Q_BLOCK:j * Q_BLOCK + window, hs]
            v = vbuf[j * Q_BLOCK:j * Q_BLOCK + window, hs]
            s = lax.dot_general(q, k, (((1,), (1,)), ((), ())), preferred_element_type=F32)
            s = s * scale + (-slopes[h]) * dist_f
            s = jnp.where(valid, s, NEG_INF)
            m = jnp.max(s, axis=-1, keepdims=True)
            p = jnp.exp(s - m)
            l = jnp.sum(p, axis=-1, keepdims=True)
            o = _dot(p.astype(BF16), v) / l
            o_ref[j * Q_BLOCK:(j + 1) * Q_BLOCK, hs] = o.astype(BF16)
            lse_tile = jnp.where(lane == h, m + jnp.log(l), lse_tile)
        lse_ref[j * Q_BLOCK:(j + 1) * Q_BLOCK, :] = lse_tile


def _attention(qkv, group):
    bn, dil, sub, _ = qkv.shape
    s = sub * dil
    rows = min(sub, ATT_ROWS)
    assert sub % rows == 0 and rows % Q_BLOCK == 0 and sub % RADIUS == 0
    n_groups = len(ATT_CONFIGS) * ATT_HEADS
    slopes = tuple(float(np.float32(2.0 ** (-8.0 * (group * ATT_HEADS + h + 1) / n_groups))) for h in range(ATT_HEADS))
    flat = qkv.reshape(bn * dil, sub, N_QKV)
    halo_per_tile = rows // RADIUS
    last_halo = sub // RADIUS - 1

    def prev_map(col):
        return lambda b, r, i: (b * dil + r, jnp.maximum(i * halo_per_tile - 1, 0), col)

    def next_map(col):
        return lambda b, r, i: (b * dil + r, jnp.minimum((i + 1) * halo_per_tile, last_halo), col)

    o, lse = pl.pallas_call(
        functools.partial(_attn_kernel, sub=sub, rows=rows, dil=dil, slopes=slopes),
        grid=(bn, dil, sub // rows),
        in_specs=[pl.BlockSpec((None, rows, N_QKV), lambda b, r, i: (b * dil + r, i, 0)),
                  pl.BlockSpec((None, RADIUS, D_WIDTH), prev_map(1)),
                  pl.BlockSpec((None, RADIUS, D_WIDTH), prev_map(2)),
                  pl.BlockSpec((None, RADIUS, D_WIDTH), next_map(1)),
                  pl.BlockSpec((None, RADIUS, D_WIDTH), next_map(2))],
        out_specs=[pl.BlockSpec((None, rows, D_WIDTH), lambda b, r, i: (b, i, r)),
                   pl.BlockSpec((None, rows, LANES), lambda b, r, i: (b, i, r))],
        out_shape=[jax.ShapeDtypeStruct((bn, sub, dil * D_WIDTH), BF16),
                   jax.ShapeDtypeStruct((bn, sub, dil * LANES), F32)],
        scratch_shapes=[pltpu.VMEM((rows + 2 * RADIUS, D_WIDTH), BF16),
                        pltpu.VMEM((rows + 2 * RADIUS, D_WIDTH), BF16)],
        compiler_params=_params(3),
        name=f"band_attention_d{dil}",
    )(flat, flat, flat, flat, flat)
    return o.reshape(bn * s, D_WIDTH), lse.reshape(bn * s, LANES)


def _mix_kernel(x_ref, xp_ref, xn_ref, o0_ref, o1_ref, o2_ref, l0_ref, l1_ref, l2_ref,
                win_ref, ws_ref, bs_ref, cw_ref, pw_ref, ps_ref, wbr_ref, wg_ref, wo_ref,
                g1_ref, b1_ref, rwh_ref, rwl_ref, rb_ref,
                x1_ref, ti_ref, tp_ref,
                xe_ref, zb_ref, cb_ref, *, rows, tiles_per_seq, seq):
    tis = pl.program_id(0) % tiles_per_seq
    has_prev = tis > 0
    has_next = tis < tiles_per_seq - 1

    x = x_ref[...]
    xe_ref[0:rows, :] = x.astype(BF16)
    xe_ref[rows:rows + 2 * HALO, :] = jnp.concatenate([xp_ref[...], xn_ref[...]], axis=0).astype(BF16)
    xm = xe_ref[0:rows, :]

    h_a = _dot(xm, win_ref[:, 0:2 * A_WIDTH])
    a_u = h_a[:, 0:A_WIDTH]
    vn = _layer_norm(h_a[:, A_WIDTH:2 * A_WIDTH]).astype(BF16)
    gw = A_WIDTH // A_GROUPS
    chunks = []
    for c in range(rows // CHUNK):
        cols = [_dot(ws_ref[g], vn[c * CHUNK:(c + 1) * CHUNK, g * gw:(g + 1) * gw]) + bs_ref[g]
                for g in range(A_GROUPS)]
        chunks.append(jnp.concatenate(cols, axis=1))
    y_a = a_u * jnp.concatenate(chunks, axis=0)

    h_bc = _dot(xe_ref[...], win_ref[:, 2 * A_WIDTH:N_ABC])
    b_b = h_bc[0:rows, B_WIDTH:2 * B_WIDTH]
    z = h_bc[:, 2 * B_WIDTH:3 * B_WIDTH] * h_bc[:, 0:B_WIDTH]
    cz = h_bc[:, 3 * B_WIDTH:3 * B_WIDTH + C_WIDTH]
    for buf, val in ((zb_ref, z), (cb_ref, cz)):
        buf[0:HALO, :] = jnp.where(has_prev, val[rows:rows + HALO], 0.0)
        buf[HALO:HALO + rows, :] = val[0:rows]
        buf[HALO + rows:, :] = jnp.where(has_next, val[rows + HALO:], 0.0)

    conv = (cw_ref[0:1, :] * zb_ref[HALO - 1:HALO - 1 + rows, :]
            + cw_ref[1:2, :] * zb_ref[HALO:HALO + rows, :]
            + cw_ref[2:3, :] * zb_ref[HALO + 1:HALO + 1 + rows, :])
    y_b = b_b * conv

    pos = tis * rows + lax.broadcasted_iota(jnp.int32, (rows, 1), 0)
    cgw = C_WIDTH // len(POOL_WINDOWS)
    pooled_out = []
    for g, w in enumerate(POOL_WINDOWS):
        cs = slice(g * cgw, (g + 1) * cgw)
        acc = None
        for dd in range(-(w // 2), w // 2):
            v = cb_ref[HALO + dd:HALO + dd + rows, cs]
            acc = v if acc is None else acc + v
        count = (jnp.minimum(pos + w // 2, seq) - jnp.maximum(pos - w // 2, 0)).astype(F32)
        pooled = acc / count - cb_ref[HALO:HALO + rows, cs]
        pooled_out.append(_dot(pooled.astype(BF16), pw_ref[g]))
    y_c = jnp.concatenate(pooled_out, axis=1) * ps_ref[...]

    lses = (l0_ref[...], l1_ref[...], l2_ref[...])
    top = jnp.maximum(jnp.maximum(lses[0], lses[1]), lses[2])
    es = [jnp.exp(l - top) for l in lses]
    den = es[0] + es[1] + es[2]
    wts = [e / den for e in es]
    o_refs = (o0_ref, o1_ref, o2_ref)
    heads = []
    for h in range(ATT_HEADS):
        hs = slice(h * HEAD_DIM, (h + 1) * HEAD_DIM)
        acc = None
        for g in range(len(ATT_CONFIGS)):
            term = wts[g][:, h:h + 1] * o_refs[g][:, hs].astype(F32)
            acc = term if acc is None else acc + term
        heads.append(acc)
    y_d = jnp.concatenate(heads, axis=1)

    merged = None
    for n, y in enumerate((y_a, y_b, y_c, y_d)):
        term = jax.nn.sigmoid(_dot(xm, wg_ref[n])) * _dot(y.astype(BF16), wbr_ref[n])
        merged = term if merged is None else merged + term
    u = ALPHA * x + _dot(merged.astype(BF16), wo_ref[...])
    x1 = _layer_norm(u) * g1_ref[...] + b1_ref[...]
    x1_ref[...] = x1

    x1h = x1.astype(BF16)
    x1l = (x1 - x1h.astype(F32)).astype(BF16)
    logits = _dot(x1h, rwh_ref[...]) + _dot(x1l, rwh_ref[...]) + _dot(x1h, rwl_ref[...]) + rb_ref[...]
    lane = lax.broadcasted_iota(jnp.int32, (rows, LANES), 1)
    lane_f = lane.astype(F32)
    vals, idxs = [], []
    work = logits
    for _ in range(TOP_K):
        m = jnp.max(work, axis=-1, keepdims=True)
        idx = jnp.min(jnp.where(work == m, lane_f, float(LANES)), axis=-1, keepdims=True)
        vals.append(m)
        idxs.append(idx)
        work = jnp.where(lane_f == idx, -jnp.inf, work)
    exps = [jnp.exp(v - vals[0]) for v in vals]
    tot = exps[0] + exps[1] + exps[2] + exps[3]
    ti = jnp.zeros((rows, LANES), F32)
    tp = jnp.zeros((rows, LANES), F32)
    for k in range(TOP_K):
        ti = jnp.where(lane == k, idxs[k], ti)
        tp = jnp.where(lane == k, exps[k] / tot, tp)
    ti_ref[...] = ti.astype(jnp.int32)
    tp_ref[...] = tp


def _mixer(x2, seq, attn, lw):
    t, d = x2.shape
    rows = MIX_ROWS
    assert seq % rows == 0 and rows % CHUNK == 0
    halo_per_tile = rows // HALO
    last_halo = t // HALO - 1
    tile = lambda w: pl.BlockSpec((rows, w), lambda i: (i, 0))
    in_specs = [tile(d),
                pl.BlockSpec((HALO, d), lambda i: (jnp.maximum(i * halo_per_tile - 1, 0), 0)),
                pl.BlockSpec((HALO, d), lambda i: (jnp.minimum((i + 1) * halo_per_tile, last_halo), 0)),
                tile(D_WIDTH), tile(D_WIDTH), tile(D_WIDTH), tile(LANES), tile(LANES), tile(LANES)]
    weights = (lw["w_abc"], lw["spatial_w"], lw["spatial_b"], lw["conv_w"], lw["pool_w"], lw["pool_scale"],
               lw["w_branch"], lw["w_gate"], lw["w_out"], lw["ln1_g"], lw["ln1_b"],
               lw["router_hi"], lw["router_lo"], lw["router_b"])
    in_specs += [_resident(w.shape) for w in weights]
    return pl.pallas_call(
        functools.partial(_mix_kernel, rows=rows, tiles_per_seq=seq // rows, seq=seq),
        grid=(t // rows,),
        in_specs=in_specs,
        out_specs=[tile(d), tile(LANES), tile(LANES)],
        out_shape=[jax.ShapeDtypeStruct((t, d), F32),
                   jax.ShapeDtypeStruct((t, LANES), jnp.int32),
                   jax.ShapeDtypeStruct((t, LANES), F32)],
        scratch_shapes=[pltpu.VMEM((rows + 2 * HALO, d), BF16),
                        pltpu.VMEM((rows + 2 * HALO, B_WIDTH), F32),
                        pltpu.VMEM((rows + 2 * HALO, C_WIDTH), F32)],
        compiler_params=_params(1),
        name="mixer",
    )(x2, x2, x2, attn[0][0], attn[1][0], attn[2][0], attn[0][1], attn[1][1], attn[2][1], *weights)


def _row_gather(src_hbm, idx_ref, dst, sem, n):
    def body(i, carry):
        pltpu.make_async_copy(src_hbm.at[pl.ds(idx_ref[0, 0, i], 1)], dst.at[pl.ds(i, 1)], sem).start()
        return carry
    lax.fori_loop(0, n, body, 0, unroll=8)


def _row_gather_wait(src_hbm, dst, sem, n):
    pltpu.make_async_copy(src_hbm.at[pl.ds(0, n)], dst, sem).wait()


def _expert_kernel(be_ref, nu_ref, tok_ref, x_hbm, wgu_ref, bgu_ref, wd_ref, bd_ref, y_ref, gbuf, gsem, *, n_blocks):
    s = pl.program_id(0)
    n_used = nu_ref[0]

    @pl.when((s < n_blocks) & (s < n_used))
    def _():
        _row_gather(x_hbm, tok_ref, gbuf.at[s % 2], gsem.at[s % 2], EXPERT_ROWS)

    b = s - 1

    @pl.when((s >= 1) & (b < n_used))
    def _():
        slot = b % 2
        _row_gather_wait(x_hbm, gbuf.at[slot], gsem.at[slot], EXPERT_ROWS)
        xg = gbuf[slot].astype(BF16)
        h = _dot(xg, wgu_ref[...]) + bgu_ref[...]
        g = jnp.minimum(h[:, :D_FF], SWIGLU_LIMIT)
        u = jnp.clip(h[:, D_FF:], -SWIGLU_LIMIT, SWIGLU_LIMIT)
        act = g * jax.nn.sigmoid(SWIGLU_ALPHA * g) * (u + 1.0)
        y_ref[...] = _dot(act.astype(BF16), wd_ref[...]) + bd_ref[...]

    @pl.when((s >= 1) & (b >= n_used))
    def _():
        y_ref[...] = jnp.zeros_like(y_ref)


def _experts(x1, slot_tok, block_e, n_used, lw):
    t, d = x1.shape
    nb = block_e.shape[0]
    m = EXPERT_ROWS
    cur = lambda s, be, nu: jnp.maximum(s - 1, 0)
    grid_spec = pltpu.PrefetchScalarGridSpec(
        num_scalar_prefetch=2,
        grid=(nb + 1,),
        in_specs=[pl.BlockSpec((1, 1, m), lambda s, be, nu: (jnp.minimum(s, nb - 1), 0, 0), memory_space=pltpu.SMEM),
                  pl.BlockSpec(memory_space=pl.ANY),
                  pl.BlockSpec((None, d, 2 * D_FF), lambda s, be, nu: (be[cur(s, be, nu)], 0, 0)),
                  pl.BlockSpec((None, 1, 2 * D_FF), lambda s, be, nu: (be[cur(s, be, nu)], 0, 0)),
                  pl.BlockSpec((None, D_FF, d), lambda s, be, nu: (be[cur(s, be, nu)], 0, 0)),
                  pl.BlockSpec((None, 1, d), lambda s, be, nu: (be[cur(s, be, nu)], 0, 0))],
        out_specs=pl.BlockSpec((m, d), lambda s, be, nu: (cur(s, be, nu), 0)),
        scratch_shapes=[pltpu.VMEM((2, m, d), F32), pltpu.SemaphoreType.DMA((2,))],
    )
    return pl.pallas_call(
        functools.partial(_expert_kernel, n_blocks=nb),
        grid_spec=grid_spec,
        out_shape=jax.ShapeDtypeStruct((nb * m, d), F32),
        compiler_params=_params(1),
        name="expert_blocks",
    )(block_e, n_used, slot_tok, x1, lw["w_gate_up"], lw["b_gate_up"], lw["w_down"], lw["b_down"])


def _combine_kernel(pos_ref, y_hbm, x1_ref, p_ref, tp_ref, wpg_ref, wpp_ref, g2_ref, b2_ref, o_ref, cbuf, csem,
                    *, n_tiles, rows):
    s = pl.program_id(0)

    @pl.when(s < n_tiles)
    def _():
        _row_gather(y_hbm, pos_ref, cbuf.at[s % 2], csem.at[s % 2], TOP_K * rows)

    @pl.when(s >= 1)
    def _():
        slot = (s - 1) % 2
        _row_gather_wait(y_hbm, cbuf.at[slot], csem.at[slot], TOP_K * rows)
        x1 = x1_ref[...]
        tp = tp_ref[...]
        moe = None
        for k in range(TOP_K):
            term = tp[:, k:k + 1] * cbuf[slot, k * rows:(k + 1) * rows, :]
            moe = term if moe is None else moe + term
        ple = jax.nn.sigmoid(_dot(x1.astype(BF16), wpg_ref[...])) * _dot(p_ref[...].astype(BF16), wpp_ref[...])
        o_ref[...] = _layer_norm(ALPHA * x1 + moe + ple) * g2_ref[...] + b2_ref[...]


def _combine(x1, p2, top_p, pos, y_sorted, lw):
    t, d = x1.shape
    rows = COMBINE_ROWS
    n_tiles = t // rows
    cur = lambda s: jnp.maximum(s - 1, 0)
    tile = lambda w: pl.BlockSpec((rows, w), lambda s: (cur(s), 0))
    weights = (lw["w_ple_gate"], lw["w_ple_proj"], lw["ln2_g"], lw["ln2_b"])
    return pl.pallas_call(
        functools.partial(_combine_kernel, n_tiles=n_tiles, rows=rows),
        grid=(n_tiles + 1,),
        in_specs=[pl.BlockSpec((1, 1, TOP_K * rows), lambda s: (jnp.minimum(s, n_tiles - 1), 0, 0),
                               memory_space=pltpu.SMEM),
                  pl.BlockSpec(memory_space=pl.ANY),
                  tile(d), tile(PLE_DIM), tile(LANES)] + [_resident(w.shape) for w in weights],
        out_specs=tile(d),
        out_shape=jax.ShapeDtypeStruct((t, d), F32),
        scratch_shapes=[pltpu.VMEM((2, TOP_K * rows, d), F32), pltpu.SemaphoreType.DMA((2,))],
        compiler_params=_params(1),
        name="combine",
    )(pos, y_sorted, x1, p2, top_p, *weights)


def _routing_tables(top_i):
    t = top_i.shape[0]
    a = t * TOP_K
    m = EXPERT_ROWS
    nb = -(-(a + N_EXPERTS * (m - 1)) // m)
    e_flat = top_i.reshape(a)
    order = jnp.argsort(e_flat, stable=True).astype(jnp.int32)
    counts = jnp.zeros((N_EXPERTS,), jnp.int32).at[e_flat].add(1)
    padded = (counts + m - 1) // m * m
    start_sorted = jnp.cumsum(counts) - counts
    ends_padded = jnp.cumsum(padded)
    start_padded = ends_padded - padded
    block_e = jnp.minimum(jnp.searchsorted(ends_padded, jnp.arange(nb, dtype=jnp.int32) * m, side="right"),
                          N_EXPERTS - 1).astype(jnp.int32)
    n_used = (ends_padded[-1] // m).astype(jnp.int32).reshape(1)
    slot = jnp.arange(nb * m, dtype=jnp.int32)
    slot_e = jnp.repeat(block_e, m)
    rank = slot - start_padded[slot_e]
    valid = (rank >= 0) & (rank < counts[slot_e])
    src = order[jnp.clip(start_sorted[slot_e] + rank, 0, a - 1)]
    slot_tok = jnp.where(valid, src // TOP_K, 0).reshape(nb, 1, m)
    e_sorted = e_flat[order]
    dest_sorted = start_padded[e_sorted] + jnp.arange(a, dtype=jnp.int32) - start_sorted[e_sorted]
    dest = jnp.zeros((a,), jnp.int32).at[order].set(dest_sorted)
    rows = COMBINE_ROWS
    pos = dest.reshape(t // rows, rows, TOP_K).transpose(0, 2, 1).reshape(t // rows, 1, TOP_K * rows)
    return slot_tok, block_e, n_used, pos


def _layer(x, p, lw):
    bn, s, d = x.shape
    x2 = x.reshape(bn * s, d)
    attn = []
    for g, (_, dil) in enumerate(ATT_CONFIGS):
        qkv = _qkv_proj(x, lw["w_qkv"][g], dil)
        attn.append(_attention(qkv, g))
    x1, top_i, top_p = _mixer(x2, s, attn, lw)
    slot_tok, block_e, n_used, pos = _routing_tables(top_i[:, :TOP_K])
    y_sorted = _experts(x1, slot_tok, block_e, n_used, lw)
    out = _combine(x1, p.reshape(bn * s, PLE_DIM), top_p, pos, y_sorted, lw)
    return out.reshape(bn, s, d)


def _prepare_layer(i, w_in, spatial_w, spatial_b, conv_w, pool_w, pool_scale, w_branch, w_gate, w_out, ln1_g, ln1_b,
                   router_w, router_b, w_gate_up, b_gate_up, w_down, b_down, w_ple_gate, w_ple_proj, ln2_g, ln2_b):
    rw = jnp.pad(router_w[i], ((0, 0), (0, LANES - N_EXPERTS)))
    rw_hi = rw.astype(BF16)
    row = lambda v: v[i].reshape(1, -1)
    return {
        "w_abc": w_in[i][:, :N_ABC].astype(BF16),
        "w_qkv": [w_in[i][:, N_ABC + g * N_QKV:N_ABC + (g + 1) * N_QKV].astype(BF16) for g in range(len(ATT_CONFIGS))],
        "spatial_w": spatial_w[i].astype(BF16),
        "spatial_b": jnp.broadcast_to(spatial_b[i][:, :, None], (A_GROUPS, CHUNK, A_WIDTH // A_GROUPS)),
        "conv_w": conv_w[i],
        "pool_w": pool_w[i].astype(BF16),
        "pool_scale": row(pool_scale),
        "w_branch": w_branch[i].astype(BF16),
        "w_gate": w_gate[i].astype(BF16),
        "w_out": w_out[i].astype(BF16),
        "ln1_g": row(ln1_g), "ln1_b": row(ln1_b),
        "router_hi": rw_hi,
        "router_lo": (rw - rw_hi.astype(F32)).astype(BF16),
        "router_b": jnp.pad(router_b[i], (0, LANES - N_EXPERTS), constant_values=NEG_INF).reshape(1, LANES),
        "w_gate_up": w_gate_up[i].astype(BF16),
        "b_gate_up": b_gate_up[i].reshape(N_EXPERTS, 1, 2 * D_FF),
        "w_down": w_down[i].astype(BF16),
        "b_down": b_down[i].reshape(N_EXPERTS, 1, D_MODEL),
        "w_ple_gate": w_ple_gate[i].astype(BF16),
        "w_ple_proj": w_ple_proj[i].astype(BF16),
        "ln2_g": row(ln2_g), "ln2_b": row(ln2_b),
    }


def _trunk(x, p, layers):
    for i, lw in enumerate(layers):
        x = _layer(x, p[i], lw)
    return x


def kernel(x_prompt, x_sample, p_prompt, p_sample, w_in, spatial_w, spatial_b, conv_w, pool_w, pool_scale, w_branch,
           w_gate, w_out, ln1_g, ln1_b, router_w, router_b, w_gate_up, b_gate_up, w_down, b_down, w_ple_gate,
           w_ple_proj, ln2_g, ln2_b):
    weights = (w_in, spatial_w, spatial_b, conv_w, pool_w, pool_scale, w_branch, w_gate, w_out, ln1_g, ln1_b,
               router_w, router_b, w_gate_up, b_gate_up, w_down, b_down, w_ple_gate, w_ple_proj, ln2_g, ln2_b)
    layers = [_prepare_layer(i, *weights) for i in range(w_in.shape[0])]
    return _trunk(x_prompt, p_prompt, layers), _trunk(x_sample, p_sample, layers)
```

```python
import functools

import numpy as np
import jax
import jax.numpy as jnp
from jax import lax
from jax.experimental import pallas as pl
from jax.experimental.pallas import tpu as pltpu

D_MODEL = 1024
PLE_DIM = 256
CHUNK = 128
A_WIDTH = 512
A_GROUPS = 4
B_WIDTH = 512
C_WIDTH = 512
POOL_WINDOWS = (2, 4, 8, 16)
ATT_CONFIGS = ((128, 1), (512, 4), (2048, 16))
ATT_HEADS = 4
HEAD_DIM = 128
D_WIDTH = ATT_HEADS * HEAD_DIM
N_ABC = 2 * A_WIDTH + 3 * B_WIDTH + C_WIDTH
N_QKV = 3 * D_WIDTH
N_EXPERTS = 32
TOP_K = 4
D_FF = 1024
SWIGLU_LIMIT = 7.0
SWIGLU_ALPHA = 1.702
DEPTH = 2
ALPHA = (2 * DEPTH) ** 0.25
LN_EPS = 1e-5
NEG_INF = -1e30

LANES = 128
HALO = 8
RADIUS = 64
Q_BLOCK = 128
PROJ_ROWS = 1024
ATT_ROWS = 512
MIX_ROWS = 512
EXPERT_ROWS = 256
COMBINE_ROWS = 512
SCATTER_DEPTH = 3
VMEM_LIMIT = 56 * 1024 * 1024

F32 = jnp.float32
BF16 = jnp.bfloat16


def _params(n_axes, vmem=VMEM_LIMIT):
    return pltpu.CompilerParams(dimension_semantics=("arbitrary",) * n_axes, vmem_limit_bytes=vmem)


def _resident(shape):
    zeros = (0,) * len(shape)
    return pl.BlockSpec(shape, lambda *_: zeros, pipeline_mode=pl.Buffered(1))


def _dot(a, b):
    return jnp.dot(a, b, preferred_element_type=F32)


def _layer_norm(x):
    mu = jnp.mean(x, axis=-1, keepdims=True)
    xc = x - mu
    var = jnp.mean(xc * xc, axis=-1, keepdims=True)
    return xc * lax.rsqrt(var + LN_EPS)


def _qkv_kernel(x_ref, w_ref, o_ref, *scratch, dil):
    rows, d = x_ref.shape
    if dil == 1:
        xe = x_ref[...].astype(BF16)
    else:
        lt_ref, xe_ref = scratch
        n = rows // dil
        for t in range(d // LANES):
            lt_ref[t] = x_ref[:, t * LANES:(t + 1) * LANES]
        for rho in range(dil):
            piece = jnp.concatenate([lt_ref[t, pl.ds(rho, n, stride=dil), :] for t in range(d // LANES)], axis=1)
            xe_ref[rho * n:(rho + 1) * n, :] = piece.astype(BF16)
        xe = xe_ref[...]
    y = _dot(xe, w_ref[...])
    o_ref[...] = y.astype(BF16).reshape(dil, rows // dil, N_QKV)


def _qkv_proj(x, w, dil):
    bn, s, d = x.shape
    rows = PROJ_ROWS
    assert s % rows == 0 and (rows // dil) % 16 == 0
    scratch = [] if dil == 1 else [pltpu.VMEM((d // LANES, rows, LANES), F32), pltpu.VMEM((rows, d), BF16)]
    return pl.pallas_call(
        functools.partial(_qkv_kernel, dil=dil),
        grid=(bn, s // rows),
        in_specs=[pl.BlockSpec((None, rows, d), lambda b, i: (b, i, 0)),
                  _resident((d, N_QKV))],
        out_specs=pl.BlockSpec((None, dil, rows // dil, N_QKV), lambda b, i: (b, 0, i, 0)),
        out_shape=jax.ShapeDtypeStruct((bn, dil, s // dil, N_QKV), BF16),
        scratch_shapes=scratch,
        compiler_params=_params(2),
        name=f"qkv_proj_d{dil}",
    )(x, w)


def _attn_kernel(main_ref, kp_ref, vp_ref, kn_ref, vn_ref, o_ref, lse_ref, kbuf, vbuf, *, sub, rows, dil, slopes):
    i = pl.program_id(2)
    kbuf[0:RADIUS, :] = kp_ref[...]
    kbuf[RADIUS:RADIUS + rows, :] = main_ref[:, D_WIDTH:2 * D_WIDTH]
    kbuf[RADIUS + rows:, :] = kn_ref[...]
    vbuf[0:RADIUS, :] = vp_ref[...]
    vbuf[RADIUS:RADIUS + rows, :] = main_ref[:, 2 * D_WIDTH:3 * D_WIDTH]
    vbuf[RADIUS + rows:, :] = vn_ref[...]

    window = Q_BLOCK + 2 * RADIUS
    row = lax.broadcasted_iota(jnp.int32, (Q_BLOCK, window), 0)
    col = lax.broadcasted_iota(jnp.int32, (Q_BLOCK, window), 1)
    dist = jnp.abs(col - RADIUS - row)
    in_band = dist <= RADIUS
    dist_f = (dist * dil).astype(F32)
    lane = lax.broadcasted_iota(jnp.int32, (Q_BLOCK, LANES), 1)
    scale = HEAD_DIM ** -0.5
    for j in range(rows // Q_BLOCK):
        kpos = i * rows + j * Q_BLOCK - RADIUS + col
        valid = in_band & (kpos >= 0) & (kpos < sub)
        lse_tile = jnp.zeros((Q_BLOCK, LANES), F32)
        for h in range(ATT_HEADS):
            hs = slice(h * HEAD_DIM, (h + 1) * HEAD_DIM)
            q = main_ref[j * Q_BLOCK:(j + 1) * Q_BLOCK, hs]
            k = kbuf[j * Q_BLOCK:j * Q_BLOCK + window, hs]
            v = vbuf[j * Q_BLOCK:j * Q_BLOCK + window, hs]
            s = lax.dot_general(q, k, (((1,), (1,)), ((), ())), preferred_element_type=F32)
            s = s * scale + (-slopes[h]) * dist_f
            s = jnp.where(valid, s, NEG_INF)
            m = jnp.max(s, axis=-1, keepdims=True)
            p = jnp.exp(s - m)
            l = jnp.sum(p, axis=-1, keepdims=True)
            o = _dot(p.astype(BF16), v) / l
            o_ref[j * Q_BLOCK:(j + 1) * Q_BLOCK, hs] = o.astype(BF16)
            lse_tile = jnp.where(lane == h, m + jnp.log(l), lse_tile)
        lse_ref[j * Q_BLOCK:(j + 1) * Q_BLOCK, :] = lse_tile


def _attention(qkv, group):
    bn, dil, sub, _ = qkv.shape
    rows = min(sub, ATT_ROWS)
    assert sub % rows == 0 and rows % Q_BLOCK == 0 and sub % RADIUS == 0
    n_groups = len(ATT_CONFIGS) * ATT_HEADS
    slopes = tuple(float(np.float32(2.0 ** (-8.0 * (group * ATT_HEADS + h + 1) / n_groups))) for h in range(ATT_HEADS))
    halo_per_tile = rows // RADIUS
    last_halo = sub // RADIUS - 1

    def prev_map(col):
        return lambda b, r, i: (b, r, jnp.maximum(i * halo_per_tile - 1, 0), col)

    def next_map(col):
        return lambda b, r, i: (b, r, jnp.minimum((i + 1) * halo_per_tile, last_halo), col)

    return pl.pallas_call(
        functools.partial(_attn_kernel, sub=sub, rows=rows, dil=dil, slopes=slopes),
        grid=(bn, dil, sub // rows),
        in_specs=[pl.BlockSpec((None, None, rows, N_QKV), lambda b, r, i: (b, r, i, 0)),
                  pl.BlockSpec((None, None, RADIUS, D_WIDTH), prev_map(1)),
                  pl.BlockSpec((None, None, RADIUS, D_WIDTH), prev_map(2)),
                  pl.BlockSpec((None, None, RADIUS, D_WIDTH), next_map(1)),
                  pl.BlockSpec((None, None, RADIUS, D_WIDTH), next_map(2))],
        out_specs=[pl.BlockSpec((None, None, rows, D_WIDTH), lambda b, r, i: (b, r, i, 0)),
                   pl.BlockSpec((None, None, rows, LANES), lambda b, r, i: (b, r, i, 0))],
        out_shape=[jax.ShapeDtypeStruct((bn, dil, sub, D_WIDTH), BF16),
                   jax.ShapeDtypeStruct((bn, dil, sub, LANES), F32)],
        scratch_shapes=[pltpu.VMEM((rows + 2 * RADIUS, D_WIDTH), BF16),
                        pltpu.VMEM((rows + 2 * RADIUS, D_WIDTH), BF16)],
        compiler_params=_params(3),
        name=f"band_attention_d{dil}",
    )(qkv, qkv, qkv, qkv, qkv)


def _mix_kernel(x_ref, xp_ref, xn_ref, o0_ref, o1_ref, o2_ref, l0_ref, l1_ref, l2_ref,
                win_ref, ws_ref, bs_ref, cw_ref, pw_ref, ps_ref, wbr_ref, wg_ref, wo_ref,
                g1_ref, b1_ref, rwh_ref, rwl_ref, rb_ref,
                x1_ref, ti_ref, tp_ref,
                xe_ref, zb_ref, cb_ref, so1_ref, sl1_ref, so2_ref, sl2_ref, *, rows, tiles_per_seq, seq):
    tis = pl.program_id(0) % tiles_per_seq
    has_prev = tis > 0
    has_next = tis < tiles_per_seq - 1

    x = x_ref[...]
    xe_ref[0:rows, :] = x.astype(BF16)
    xe_ref[rows:rows + 2 * HALO, :] = jnp.concatenate([xp_ref[...], xn_ref[...]], axis=0).astype(BF16)
    xm = xe_ref[0:rows, :]

    h_a = _dot(xm, win_ref[:, 0:2 * A_WIDTH])
    a_u = h_a[:, 0:A_WIDTH]
    vn = _layer_norm(h_a[:, A_WIDTH:2 * A_WIDTH]).astype(BF16)
    gw = A_WIDTH // A_GROUPS
    chunks = []
    for c in range(rows // CHUNK):
        cols = [_dot(ws_ref[g], vn[c * CHUNK:(c + 1) * CHUNK, g * gw:(g + 1) * gw]) + bs_ref[g]
                for g in range(A_GROUPS)]
        chunks.append(jnp.concatenate(cols, axis=1))
    y_a = a_u * jnp.concatenate(chunks, axis=0)

    h_bc = _dot(xe_ref[...], win_ref[:, 2 * A_WIDTH:N_ABC])
    b_b = h_bc[0:rows, B_WIDTH:2 * B_WIDTH]
    z = h_bc[:, 2 * B_WIDTH:3 * B_WIDTH] * h_bc[:, 0:B_WIDTH]
    cz = h_bc[:, 3 * B_WIDTH:3 * B_WIDTH + C_WIDTH]
    for buf, val in ((zb_ref, z), (cb_ref, cz)):
        buf[0:HALO, :] = jnp.where(has_prev, val[rows:rows + HALO], 0.0)
        buf[HALO:HALO + rows, :] = val[0:rows]
        buf[HALO + rows:, :] = jnp.where(has_next, val[rows + HALO:], 0.0)

    conv = (cw_ref[0:1, :] * zb_ref[HALO - 1:HALO - 1 + rows, :]
            + cw_ref[1:2, :] * zb_ref[HALO:HALO + rows, :]
            + cw_ref[2:3, :] * zb_ref[HALO + 1:HALO + 1 + rows, :])
    y_b = b_b * conv

    pos = tis * rows + lax.broadcasted_iota(jnp.int32, (rows, 1), 0)
    cgw = C_WIDTH // len(POOL_WINDOWS)
    pooled_out = []
    for g, w in enumerate(POOL_WINDOWS):
        cs = slice(g * cgw, (g + 1) * cgw)
        acc = None
        for dd in range(-(w // 2), w // 2):
            v = cb_ref[HALO + dd:HALO + dd + rows, cs]
            acc = v if acc is None else acc + v
        count = (jnp.minimum(pos + w // 2, seq) - jnp.maximum(pos - w // 2, 0)).astype(F32)
        pooled = acc / count - cb_ref[HALO:HALO + rows, cs]
        pooled_out.append(_dot(pooled.astype(BF16), pw_ref[g]))
    y_c = jnp.concatenate(pooled_out, axis=1) * ps_ref[...]

    def token_order(o_ref, l_ref, so_ref, sl_ref):
        dil = o_ref.shape[0]
        if dil == 1:
            return [o_ref[0, :, h * HEAD_DIM:(h + 1) * HEAD_DIM].astype(F32) for h in range(ATT_HEADS)], l_ref[0]
        n = rows // dil
        for rho in range(dil):
            v = o_ref[rho].astype(F32)
            for h in range(ATT_HEADS):
                so_ref[h, pl.ds(rho, n, stride=dil), :] = v[:, h * HEAD_DIM:(h + 1) * HEAD_DIM]
            sl_ref[pl.ds(rho, n, stride=dil), :] = l_ref[rho]
        return [so_ref[h] for h in range(ATT_HEADS)], sl_ref[...]

    outs, lses = zip(token_order(o0_ref, l0_ref, None, None),
                     token_order(o1_ref, l1_ref, so1_ref, sl1_ref),
                     token_order(o2_ref, l2_ref, so2_ref, sl2_ref))
    top = jnp.maximum(jnp.maximum(lses[0], lses[1]), lses[2])
    es = [jnp.exp(l - top) for l in lses]
    den = es[0] + es[1] + es[2]
    wts = [e / den for e in es]
    heads = []
    for h in range(ATT_HEADS):
        acc = None
        for g in range(len(ATT_CONFIGS)):
            term = wts[g][:, h:h + 1] * outs[g][h]
            acc = term if acc is None else acc + term
        heads.append(acc)
    y_d = jnp.concatenate(heads, axis=1)

    merged = None
    for n, y in enumerate((y_a, y_b, y_c, y_d)):
        term = jax.nn.sigmoid(_dot(xm, wg_ref[n])) * _dot(y.astype(BF16), wbr_ref[n])
        merged = term if merged is None else merged + term
    u = ALPHA * x + _dot(merged.astype(BF16), wo_ref[...])
    x1 = _layer_norm(u) * g1_ref[...] + b1_ref[...]
    x1_ref[...] = x1

    x1h = x1.astype(BF16)
    x1l = (x1 - x1h.astype(F32)).astype(BF16)
    logits = _dot(x1h, rwh_ref[...]) + _dot(x1l, rwh_ref[...]) + _dot(x1h, rwl_ref[...]) + rb_ref[...]
    lane = lax.broadcasted_iota(jnp.int32, (rows, LANES), 1)
    lane_f = lane.astype(F32)
    vals, idxs = [], []
    work = logits
    for _ in range(TOP_K):
        m = jnp.max(work, axis=-1, keepdims=True)
        idx = jnp.min(jnp.where(work == m, lane_f, float(LANES)), axis=-1, keepdims=True)
        vals.append(m)
        idxs.append(idx)
        work = jnp.where(lane_f == idx, -jnp.inf, work)
    exps = [jnp.exp(v - vals[0]) for v in vals]
    tot = exps[0] + exps[1] + exps[2] + exps[3]
    ti = jnp.zeros((rows, LANES), F32)
    tp = jnp.zeros((rows, LANES), F32)
    for k in range(TOP_K):
        ti = jnp.where(lane == k, idxs[k], ti)
        tp = jnp.where(lane == k, exps[k] / tot, tp)
    ti_ref[...] = ti.astype(jnp.int32)
    tp_ref[...] = tp


def _mixer(x2, seq, attn, lw):
    t, d = x2.shape
    rows = MIX_ROWS
    tps = seq // rows
    assert seq % rows == 0 and rows % CHUNK == 0
    halo_per_tile = rows // HALO
    last_halo = t // HALO - 1
    tile = lambda w: pl.BlockSpec((rows, w), lambda i: (i, 0))

    def sub_tile(arr):
        _, dil, _, w = arr.shape
        assert (rows // dil) % 16 == 0
        return pl.BlockSpec((None, dil, rows // dil, w), lambda i: (i // tps, 0, i % tps, 0))

    in_specs = [tile(d),
                pl.BlockSpec((HALO, d), lambda i: (jnp.maximum(i * halo_per_tile - 1, 0), 0)),
                pl.BlockSpec((HALO, d), lambda i: (jnp.minimum((i + 1) * halo_per_tile, last_halo), 0))]
    in_specs += [sub_tile(a[0]) for a in attn] + [sub_tile(a[1]) for a in attn]
    weights = (lw["w_abc"], lw["spatial_w"], lw["spatial_b"], lw["conv_w"], lw["pool_w"], lw["pool_scale"],
               lw["w_branch"], lw["w_gate"], lw["w_out"], lw["ln1_g"], lw["ln1_b"],
               lw["router_hi"], lw["router_lo"], lw["router_b"])
    in_specs += [_resident(w.shape) for w in weights]
    interleave = [pltpu.VMEM((ATT_HEADS, rows, HEAD_DIM), F32), pltpu.VMEM((rows, LANES), F32)]
    return pl.pallas_call(
        functools.partial(_mix_kernel, rows=rows, tiles_per_seq=tps, seq=seq),
        grid=(t // rows,),
        in_specs=in_specs,
        out_specs=[tile(d), tile(LANES), tile(LANES)],
        out_shape=[jax.ShapeDtypeStruct((t, d), F32),
                   jax.ShapeDtypeStruct((t, LANES), jnp.int32),
                   jax.ShapeDtypeStruct((t, LANES), F32)],
        scratch_shapes=[pltpu.VMEM((rows + 2 * HALO, d), BF16),
                        pltpu.VMEM((rows + 2 * HALO, B_WIDTH), F32),
                        pltpu.VMEM((rows + 2 * HALO, C_WIDTH), F32)] + interleave + interleave,
        compiler_params=_params(1),
        name="mixer",
    )(x2, x2, x2, attn[0][0], attn[1][0], attn[2][0], attn[0][1], attn[1][1], attn[2][1], *weights)


def _gather_copy(x_hbm, idx_ref, gbuf, sem, i):
    return pltpu.make_async_copy(x_hbm.at[pl.ds(idx_ref[0, 0, i], 1)], gbuf.at[pl.ds(i, 1)], sem)


def _scatter_copy(obuf, idx_ref, y_hbm, sem, i):
    return pltpu.make_async_copy(obuf.at[pl.ds(i, 1)], y_hbm.at[pl.ds(idx_ref[0, 0, i], 1)], sem)


def _expert_kernel(be_ref, nu_ref, tok_ref, dst_ref, x_hbm, wgu_ref, bgu_ref, wd_ref, bd_ref, y_hbm,
                   gbuf, obuf, gsem, osem, *, n_blocks):
    m = EXPERT_ROWS
    s = pl.program_id(0)
    n_used = nu_ref[0]
    b = s - 1

    def scatter_wait(blk):
        slot = blk % 2
        pltpu.make_async_copy(obuf.at[slot], y_hbm.at[pl.ds(0, m)], osem.at[slot]).wait()

    @pl.when(s == 0)
    def _():
        obuf[0] = jnp.zeros((m, obuf.shape[2]), F32)
        n_rows = y_hbm.shape[0]
        for j in range(SCATTER_DEPTH):
            dump = pltpu.make_async_copy(obuf.at[0], y_hbm.at[pl.ds(n_rows - (j + 1) * m, m)], osem.at[0])
            dump.start()
            dump.wait()

    @pl.when((b >= 2) & (b - 2 < n_used))
    def _():
        scatter_wait(b - 2)

    @pl.when((s < n_blocks) & (s < n_used))
    def _():
        for i in range(m):
            _gather_copy(x_hbm, tok_ref, gbuf.at[s % 2], gsem.at[s % 2], i).start()

    @pl.when((b >= 0) & (b < n_used))
    def _():
        slot = b % 2
        pltpu.make_async_copy(x_hbm.at[pl.ds(0, m)], gbuf.at[slot], gsem.at[slot]).wait()
        xg = gbuf[slot].astype(BF16)
        h = _dot(xg, wgu_ref[...]) + bgu_ref[...]
        g = jnp.minimum(h[:, :D_FF], SWIGLU_LIMIT)
        u = jnp.clip(h[:, D_FF:], -SWIGLU_LIMIT, SWIGLU_LIMIT)
        act = g * jax.nn.sigmoid(SWIGLU_ALPHA * g) * (u + 1.0)
        obuf[slot] = _dot(act.astype(BF16), wd_ref[...]) + bd_ref[...]
        for i in range(m):
            _scatter_copy(obuf.at[slot], dst_ref, y_hbm, osem.at[slot], i).start()

    @pl.when(s == n_blocks)
    def _():
        @pl.when((b >= 1) & (b - 1 < n_used))
        def _():
            scatter_wait(b - 1)

        @pl.when(b < n_used)
        def _():
            scatter_wait(b)


def _experts(x1, slot_tok, slot_dst, block_e, n_used, lw):
    t, d = x1.shape
    nb = block_e.shape[0]
    m = EXPERT_ROWS
    cur = lambda s: jnp.maximum(s - 1, 0)
    grid_spec = pltpu.PrefetchScalarGridSpec(
        num_scalar_prefetch=2,
        grid=(nb + 1,),
        in_specs=[pl.BlockSpec((1, 1, m), lambda s, be, nu: (jnp.minimum(s, nb - 1), 0, 0), memory_space=pltpu.SMEM),
                  pl.BlockSpec((1, 1, m), lambda s, be, nu: (cur(s), 0, 0), memory_space=pltpu.SMEM),
                  pl.BlockSpec(memory_space=pl.ANY),
                  pl.BlockSpec((None, d, 2 * D_FF), lambda s, be, nu: (be[cur(s)], 0, 0)),
                  pl.BlockSpec((None, 1, 2 * D_FF), lambda s, be, nu: (be[cur(s)], 0, 0)),
                  pl.BlockSpec((None, D_FF, d), lambda s, be, nu: (be[cur(s)], 0, 0)),
                  pl.BlockSpec((None, 1, d), lambda s, be, nu: (be[cur(s)], 0, 0))],
        out_specs=pl.BlockSpec(memory_space=pl.ANY),
        scratch_shapes=[pltpu.VMEM((2, m, d), F32), pltpu.VMEM((2, m, d), F32),
                        pltpu.SemaphoreType.DMA((2,)), pltpu.SemaphoreType.DMA((2,))],
    )
    return pl.pallas_call(
        functools.partial(_expert_kernel, n_blocks=nb),
        grid_spec=grid_spec,
        out_shape=jax.ShapeDtypeStruct((TOP_K * t + SCATTER_DEPTH * m, d), F32),
        compiler_params=_params(1),
        name="expert_blocks",
    )(block_e, n_used, slot_tok, slot_dst, x1, lw["w_gate_up"], lw["b_gate_up"], lw["w_down"], lw["b_down"])


def _combine_kernel(y0_ref, y1_ref, y2_ref, y3_ref, x1_ref, p_ref, tp_ref, wpg_ref, wpp_ref, g2_ref, b2_ref, o_ref):
    x1 = x1_ref[...]
    tp = tp_ref[...]
    moe = None
    for k, y_ref in enumerate((y0_ref, y1_ref, y2_ref, y3_ref)):
        term = tp[:, k:k + 1] * y_ref[...]
        moe = term if moe is None else moe + term
    ple = jax.nn.sigmoid(_dot(x1.astype(BF16), wpg_ref[...])) * _dot(p_ref[...].astype(BF16), wpp_ref[...])
    o_ref[...] = _layer_norm(ALPHA * x1 + moe + ple) * g2_ref[...] + b2_ref[...]


def _combine(x1, p2, top_p, y, lw):
    t, d = x1.shape
    rows = COMBINE_ROWS
    n_tiles = t // rows
    tile = lambda w: pl.BlockSpec((rows, w), lambda i: (i, 0))
    choice = lambda k: pl.BlockSpec((rows, d), lambda i: (k * n_tiles + i, 0))
    weights = (lw["w_ple_gate"], lw["w_ple_proj"], lw["ln2_g"], lw["ln2_b"])
    return pl.pallas_call(
        _combine_kernel,
        grid=(n_tiles,),
        in_specs=[choice(k) for k in range(TOP_K)] + [tile(d), tile(PLE_DIM), tile(LANES)]
        + [_resident(w.shape) for w in weights],
        out_specs=tile(d),
        out_shape=jax.ShapeDtypeStruct((t, d), F32),
        compiler_params=_params(1),
        name="combine",
    )(y, y, y, y, x1, p2, top_p, *weights)


def _routing_tables(top_i):
    t = top_i.shape[0]
    a = t * TOP_K
    m = EXPERT_ROWS
    nb = -(-(a + N_EXPERTS * (m - 1)) // m)
    e_flat = top_i.reshape(a)
    keys = jnp.sort(e_flat * a + jnp.arange(a, dtype=jnp.int32))
    a_sorted = keys % a
    experts = jnp.arange(N_EXPERTS, dtype=jnp.int32)
    start_sorted = jnp.sum(keys[None, :] < (experts * a)[:, None], axis=1, dtype=jnp.int32)
    counts = jnp.concatenate([start_sorted[1:], jnp.full((1,), a, jnp.int32)]) - start_sorted
    padded = (counts + m - 1) // m * m
    ends_padded = jnp.cumsum(padded)
    start_padded = ends_padded - padded
    block_start = jnp.arange(nb, dtype=jnp.int32) * m
    block_e = jnp.minimum(jnp.sum(block_start[:, None] >= ends_padded[None, :], axis=1, dtype=jnp.int32),
                          N_EXPERTS - 1)
    n_used = (ends_padded[-1] // m).astype(jnp.int32).reshape(1)
    within = jnp.arange(m, dtype=jnp.int32)[None, :]
    rank = block_start[:, None] + within - start_padded[block_e][:, None]
    valid = (rank >= 0) & (rank < counts[block_e][:, None])
    src = a_sorted[jnp.clip(start_sorted[block_e][:, None] + rank, 0, a - 1)]
    slot_tok = jnp.where(valid, src // TOP_K, 0)
    dump = a + (jnp.arange(nb, dtype=jnp.int32) % SCATTER_DEPTH)[:, None] * m + within
    slot_dst = jnp.where(valid, (src % TOP_K) * t + src // TOP_K, dump)
    return slot_tok.reshape(nb, 1, m), slot_dst.reshape(nb, 1, m), block_e, n_used


def _layer(x, p, lw):
    bn, s, d = x.shape
    x2 = x.reshape(bn * s, d)
    attn = []
    for g, (_, dil) in enumerate(ATT_CONFIGS):
        qkv = _qkv_proj(x, lw["w_qkv"][g], dil)
        attn.append(_attention(qkv, g))
    x1, top_i, top_p = _mixer(x2, s, attn, lw)
    slot_tok, slot_dst, block_e, n_used = _routing_tables(top_i[:, :TOP_K])
    y = _experts(x1, slot_tok, slot_dst, block_e, n_used, lw)
    out = _combine(x1, p.reshape(bn * s, PLE_DIM), top_p, y, lw)
    return out.reshape(bn, s, d)


def _prepare_layer(i, w_in, spatial_w, spatial_b, conv_w, pool_w, pool_scale, w_branch, w_gate, w_out, ln1_g, ln1_b,
                   router_w, router_b, w_gate_up, b_gate_up, w_down, b_down, w_ple_gate, w_ple_proj, ln2_g, ln2_b):
    rw = jnp.pad(router_w[i], ((0, 0), (0, LANES - N_EXPERTS)))
    rw_hi = rw.astype(BF16)
    row = lambda v: v[i].reshape(1, -1)
    return {
        "w_abc": w_in[i][:, :N_ABC].astype(BF16),
        "w_qkv": [w_in[i][:, N_ABC + g * N_QKV:N_ABC + (g + 1) * N_QKV].astype(BF16) for g in range(len(ATT_CONFIGS))],
        "spatial_w": spatial_w[i].astype(BF16),
        "spatial_b": jnp.broadcast_to(spatial_b[i][:, :, None], (A_GROUPS, CHUNK, A_WIDTH // A_GROUPS)),
        "conv_w": conv_w[i],
        "pool_w": pool_w[i].astype(BF16),
        "pool_scale": row(pool_scale),
        "w_branch": w_branch[i].astype(BF16),
        "w_gate": w_gate[i].astype(BF16),
        "w_out": w_out[i].astype(BF16),
        "ln1_g": row(ln1_g), "ln1_b": row(ln1_b),
        "router_hi": rw_hi,
        "router_lo": (rw - rw_hi.astype(F32)).astype(BF16),
        "router_b": jnp.pad(router_b[i], (0, LANES - N_EXPERTS), constant_values=NEG_INF).reshape(1, LANES),
        "w_gate_up": w_gate_up[i].astype(BF16),
        "b_gate_up": b_gate_up[i].reshape(N_EXPERTS, 1, 2 * D_FF),
        "w_down": w_down[i].astype(BF16),
        "b_down": b_down[i].reshape(N_EXPERTS, 1, D_MODEL),
        "w_ple_gate": w_ple_gate[i].astype(BF16),
        "w_ple_proj": w_ple_proj[i].astype(BF16),
        "ln2_g": row(ln2_g), "ln2_b": row(ln2_b),
    }


def _trunk(x, p, layers):
    for i, lw in enumerate(layers):
        x = _layer(x, p[i], lw)
    return x


def kernel(x_prompt, x_sample, p_prompt, p_sample, w_in, spatial_w, spatial_b, conv_w, pool_w, pool_scale, w_branch,
           w_gate, w_out, ln1_g, ln1_b, router_w, router_b, w_gate_up, b_gate_up, w_down, b_down, w_ple_gate,
           w_ple_proj, ln2_g, ln2_b):
    weights = (w_in, spatial_w, spatial_b, conv_w, pool_w, pool_scale, w_branch, w_gate, w_out, ln1_g, ln1_b,
               router_w, router_b, w_gate_up, b_gate_up, w_down, b_down, w_ple_gate, w_ple_proj, ln2_g, ln2_b)
    layers = [_prepare_layer(i, *weights) for i in range(w_in.shape[0])]
    return _trunk(x_prompt, p_prompt, layers), _trunk(x_sample, p_sample, layers)
```

```python
import functools

import numpy as np
import jax
import jax.numpy as jnp
from jax import lax
from jax.experimental import pallas as pl
from jax.experimental.pallas import tpu as pltpu

D_MODEL = 1024
PLE_DIM = 256
CHUNK = 128
A_WIDTH = 512
A_GROUPS = 4
B_WIDTH = 512
C_WIDTH = 512
POOL_WINDOWS = (2, 4, 8, 16)
ATT_CONFIGS = ((128, 1), (512, 4), (2048, 16))
ATT_HEADS = 4
HEAD_DIM = 128
D_WIDTH = ATT_HEADS * HEAD_DIM
N_ABC = 2 * A_WIDTH + 3 * B_WIDTH + C_WIDTH
N_QKV = 3 * D_WIDTH
N_EXPERTS = 32
TOP_K = 4
D_FF = 1024
SWIGLU_LIMIT = 7.0
SWIGLU_ALPHA = 1.702
DEPTH = 2
ALPHA = (2 * DEPTH) ** 0.25
LN_EPS = 1e-5
NEG_INF = -1e30

LANES = 128
SUBLANES = 8
HALO = 8
RADIUS = 64
Q_BLOCK = 128
PROJ_ROWS = 1024
ATT_ROWS = 512
MIX_ROWS = 512
EXPERT_ROWS = 256
COMBINE_ROWS = 512
SCATTER_DEPTH = 3
VMEM_LIMIT = 56 * 1024 * 1024

F32 = jnp.float32
BF16 = jnp.bfloat16


def _params(n_axes, vmem=VMEM_LIMIT):
    return pltpu.CompilerParams(dimension_semantics=("arbitrary",) * n_axes, vmem_limit_bytes=vmem)


def _resident(shape):
    zeros = (0,) * len(shape)
    return pl.BlockSpec(shape, lambda *_: zeros, pipeline_mode=pl.Buffered(1))


def _dot(a, b):
    return jnp.dot(a, b, preferred_element_type=F32)


def _layer_norm(x):
    mu = jnp.mean(x, axis=-1, keepdims=True)
    xc = x - mu
    var = jnp.mean(xc * xc, axis=-1, keepdims=True)
    return xc * lax.rsqrt(var + LN_EPS)


def _qkv_kernel(x_ref, w_ref, o_ref, *scratch, dil):
    rows, d = x_ref.shape
    if dil == 1:
        xe = x_ref[...].astype(BF16)
    else:
        lt_ref, xe_ref = scratch
        n = rows // dil
        for t in range(d // LANES):
            lt_ref[t] = x_ref[:, t * LANES:(t + 1) * LANES]
        for rho in range(dil):
            piece = jnp.concatenate([lt_ref[t, pl.ds(rho, n, stride=dil), :] for t in range(d // LANES)], axis=1)
            xe_ref[rho * n:(rho + 1) * n, :] = piece.astype(BF16)
        xe = xe_ref[...]
    y = _dot(xe, w_ref[...])
    o_ref[...] = y.astype(BF16).reshape(dil, rows // dil, N_QKV)


def _qkv_proj(x, w, dil):
    bn, s, d = x.shape
    rows = PROJ_ROWS
    assert s % rows == 0 and (rows // dil) % 16 == 0
    scratch = [] if dil == 1 else [pltpu.VMEM((d // LANES, rows, LANES), F32), pltpu.VMEM((rows, d), BF16)]
    return pl.pallas_call(
        functools.partial(_qkv_kernel, dil=dil),
        grid=(bn, s // rows),
        in_specs=[pl.BlockSpec((None, rows, d), lambda b, i: (b, i, 0)),
                  _resident((d, N_QKV))],
        out_specs=pl.BlockSpec((None, dil, rows // dil, N_QKV), lambda b, i: (b, 0, i, 0)),
        out_shape=jax.ShapeDtypeStruct((bn, dil, s // dil, N_QKV), BF16),
        scratch_shapes=scratch,
        compiler_params=_params(2),
        name=f"qkv_proj_d{dil}",
    )(x, w)


def _attn_kernel(main_ref, kp_ref, vp_ref, kn_ref, vn_ref, o_ref, lse_ref, kbuf, vbuf, *, sub, rows, dil, slopes):
    i = pl.program_id(2)
    kbuf[0:RADIUS, :] = kp_ref[...]
    kbuf[RADIUS:RADIUS + rows, :] = main_ref[:, D_WIDTH:2 * D_WIDTH]
    kbuf[RADIUS + rows:, :] = kn_ref[...]
    vbuf[0:RADIUS, :] = vp_ref[...]
    vbuf[RADIUS:RADIUS + rows, :] = main_ref[:, 2 * D_WIDTH:3 * D_WIDTH]
    vbuf[RADIUS + rows:, :] = vn_ref[...]

    window = Q_BLOCK + 2 * RADIUS
    row = lax.broadcasted_iota(jnp.int32, (Q_BLOCK, window), 0)
    col = lax.broadcasted_iota(jnp.int32, (Q_BLOCK, window), 1)
    dist = jnp.abs(col - RADIUS - row)
    in_band = dist <= RADIUS
    dist_f = (dist * dil).astype(F32)
    lane = lax.broadcasted_iota(jnp.int32, (Q_BLOCK, LANES), 1)
    scale = HEAD_DIM ** -0.5
    for j in range(rows // Q_BLOCK):
        kpos = i * rows + j * Q_BLOCK - RADIUS + col
        valid = in_band & (kpos >= 0) & (kpos < sub)
        lse_tile = jnp.zeros((Q_BLOCK, LANES), F32)
        for h in range(ATT_HEADS):
            hs = slice(h * HEAD_DIM, (h + 1) * HEAD_DIM)
            q = main_ref[j * Q_BLOCK:(j + 1) * Q_BLOCK, hs]
            k = kbuf[j * Q_BLOCK:j * Q_BLOCK + window, hs]
            v = vbuf[j * Q_BLOCK:j * Q_BLOCK + window, hs]
            s = lax.dot_general(q, k, (((1,), (1,)), ((), ())), preferred_element_type=F32)
            s = s * scale + (-slopes[h]) * dist_f
            s = jnp.where(valid, s, NEG_INF)
            m = jnp.max(s, axis=-1, keepdims=True)
            p = jnp.exp(s - m)
            l = jnp.sum(p, axis=-1, keepdims=True)
            o = _dot(p.astype(BF16), v) / l
            o_ref[j * Q_BLOCK:(j + 1) * Q_BLOCK, hs] = o.astype(BF16)
            lse_tile = jnp.where(lane == h, m + jnp.log(l), lse_tile)
        lse_ref[j * Q_BLOCK:(j + 1) * Q_BLOCK, :] = lse_tile


def _attention(qkv, group):
    bn, dil, sub, _ = qkv.shape
    rows = min(sub, ATT_ROWS)
    assert sub % rows == 0 and rows % Q_BLOCK == 0 and sub % RADIUS == 0
    n_groups = len(ATT_CONFIGS) * ATT_HEADS
    slopes = tuple(float(np.float32(2.0 ** (-8.0 * (group * ATT_HEADS + h + 1) / n_groups))) for h in range(ATT_HEADS))
    halo_per_tile = rows // RADIUS
    last_halo = sub // RADIUS - 1

    def prev_map(col):
        return lambda b, r, i: (b, r, jnp.maximum(i * halo_per_tile - 1, 0), col)

    def next_map(col):
        return lambda b, r, i: (b, r, jnp.minimum((i + 1) * halo_per_tile, last_halo), col)

    return pl.pallas_call(
        functools.partial(_attn_kernel, sub=sub, rows=rows, dil=dil, slopes=slopes),
        grid=(bn, dil, sub // rows),
        in_specs=[pl.BlockSpec((None, None, rows, N_QKV), lambda b, r, i: (b, r, i, 0)),
                  pl.BlockSpec((None, None, RADIUS, D_WIDTH), prev_map(1)),
                  pl.BlockSpec((None, None, RADIUS, D_WIDTH), prev_map(2)),
                  pl.BlockSpec((None, None, RADIUS, D_WIDTH), next_map(1)),
                  pl.BlockSpec((None, None, RADIUS, D_WIDTH), next_map(2))],
        out_specs=[pl.BlockSpec((None, None, rows, D_WIDTH), lambda b, r, i: (b, r, i, 0)),
                   pl.BlockSpec((None, None, rows, LANES), lambda b, r, i: (b, r, i, 0))],
        out_shape=[jax.ShapeDtypeStruct((bn, dil, sub, D_WIDTH), BF16),
                   jax.ShapeDtypeStruct((bn, dil, sub, LANES), F32)],
        scratch_shapes=[pltpu.VMEM((rows + 2 * RADIUS, D_WIDTH), BF16),
                        pltpu.VMEM((rows + 2 * RADIUS, D_WIDTH), BF16)],
        compiler_params=_params(3),
        name=f"band_attention_d{dil}",
    )(qkv, qkv, qkv, qkv, qkv)


def _mix_kernel(x_ref, xp_ref, xn_ref, o0_ref, o1_ref, o2_ref, l0_ref, l1_ref, l2_ref,
                win_ref, ws_ref, bs_ref, cw_ref, pw_ref, ps_ref, wbr_ref, wg_ref, wo_ref,
                g1_ref, b1_ref, rwh_ref, rwl_ref, rb_ref,
                x1_ref, ti_ref, tp_ref,
                xe_ref, zb_ref, cb_ref, so1_ref, sl1_ref, so2_ref, sl2_ref, *, rows, tiles_per_seq, seq):
    tis = pl.program_id(0) % tiles_per_seq
    has_prev = tis > 0
    has_next = tis < tiles_per_seq - 1

    x = x_ref[...]
    xe_ref[0:rows, :] = x.astype(BF16)
    xe_ref[rows:rows + 2 * HALO, :] = jnp.concatenate([xp_ref[...], xn_ref[...]], axis=0).astype(BF16)
    xm = xe_ref[0:rows, :]

    h_a = _dot(xm, win_ref[:, 0:2 * A_WIDTH])
    a_u = h_a[:, 0:A_WIDTH]
    vn = _layer_norm(h_a[:, A_WIDTH:2 * A_WIDTH]).astype(BF16)
    gw = A_WIDTH // A_GROUPS
    chunks = []
    for c in range(rows // CHUNK):
        cols = [_dot(ws_ref[g], vn[c * CHUNK:(c + 1) * CHUNK, g * gw:(g + 1) * gw]) + bs_ref[g]
                for g in range(A_GROUPS)]
        chunks.append(jnp.concatenate(cols, axis=1))
    y_a = a_u * jnp.concatenate(chunks, axis=0)

    h_bc = _dot(xe_ref[...], win_ref[:, 2 * A_WIDTH:N_ABC])
    b_b = h_bc[0:rows, B_WIDTH:2 * B_WIDTH]
    z = h_bc[:, 2 * B_WIDTH:3 * B_WIDTH] * h_bc[:, 0:B_WIDTH]
    cz = h_bc[:, 3 * B_WIDTH:3 * B_WIDTH + C_WIDTH]
    for buf, val in ((zb_ref, z), (cb_ref, cz)):
        buf[0:HALO, :] = jnp.where(has_prev, val[rows:rows + HALO], 0.0)
        buf[HALO:HALO + rows, :] = val[0:rows]
        buf[HALO + rows:, :] = jnp.where(has_next, val[rows + HALO:], 0.0)

    conv = (cw_ref[0:1, :] * zb_ref[HALO - 1:HALO - 1 + rows, :]
            + cw_ref[1:2, :] * zb_ref[HALO:HALO + rows, :]
            + cw_ref[2:3, :] * zb_ref[HALO + 1:HALO + 1 + rows, :])
    y_b = b_b * conv

    pos = tis * rows + lax.broadcasted_iota(jnp.int32, (rows, 1), 0)
    cgw = C_WIDTH // len(POOL_WINDOWS)
    pooled_out = []
    for g, w in enumerate(POOL_WINDOWS):
        cs = slice(g * cgw, (g + 1) * cgw)
        acc = None
        for dd in range(-(w // 2), w // 2):
            v = cb_ref[HALO + dd:HALO + dd + rows, cs]
            acc = v if acc is None else acc + v
        count = (jnp.minimum(pos + w // 2, seq) - jnp.maximum(pos - w // 2, 0)).astype(F32)
        pooled = acc / count - cb_ref[HALO:HALO + rows, cs]
        pooled_out.append(_dot(pooled.astype(BF16), pw_ref[g]))
    y_c = jnp.concatenate(pooled_out, axis=1) * ps_ref[...]

    def token_order(o_ref, l_ref, so_ref, sl_ref):
        dil = o_ref.shape[0]
        if dil == 1:
            return [o_ref[0, :, h * HEAD_DIM:(h + 1) * HEAD_DIM].astype(F32) for h in range(ATT_HEADS)], l_ref[0]
        n = rows // dil
        for rho in range(dil):
            v = o_ref[rho].astype(F32)
            for h in range(ATT_HEADS):
                so_ref[h, pl.ds(rho, n, stride=dil), :] = v[:, h * HEAD_DIM:(h + 1) * HEAD_DIM]
            sl_ref[pl.ds(rho, n, stride=dil), :] = l_ref[rho]
        return [so_ref[h] for h in range(ATT_HEADS)], sl_ref[...]

    outs, lses = zip(token_order(o0_ref, l0_ref, None, None),
                     token_order(o1_ref, l1_ref, so1_ref, sl1_ref),
                     token_order(o2_ref, l2_ref, so2_ref, sl2_ref))
    top = jnp.maximum(jnp.maximum(lses[0], lses[1]), lses[2])
    es = [jnp.exp(l - top) for l in lses]
    den = es[0] + es[1] + es[2]
    wts = [e / den for e in es]
    heads = []
    for h in range(ATT_HEADS):
        acc = None
        for g in range(len(ATT_CONFIGS)):
            term = wts[g][:, h:h + 1] * outs[g][h]
            acc = term if acc is None else acc + term
        heads.append(acc)
    y_d = jnp.concatenate(heads, axis=1)

    merged = None
    for n, y in enumerate((y_a, y_b, y_c, y_d)):
        term = jax.nn.sigmoid(_dot(xm, wg_ref[n])) * _dot(y.astype(BF16), wbr_ref[n])
        merged = term if merged is None else merged + term
    u = ALPHA * x + _dot(merged.astype(BF16), wo_ref[...])
    x1 = _layer_norm(u) * g1_ref[...] + b1_ref[...]
    x1_ref[...] = x1.reshape(rows, SUBLANES, LANES)

    x1h = x1.astype(BF16)
    x1l = (x1 - x1h.astype(F32)).astype(BF16)
    logits = _dot(x1h, rwh_ref[...]) + _dot(x1l, rwh_ref[...]) + _dot(x1h, rwl_ref[...]) + rb_ref[...]
    lane = lax.broadcasted_iota(jnp.int32, (rows, LANES), 1)
    lane_f = lane.astype(F32)
    vals, idxs = [], []
    work = logits
    for _ in range(TOP_K):
        m = jnp.max(work, axis=-1, keepdims=True)
        idx = jnp.min(jnp.where(work == m, lane_f, float(LANES)), axis=-1, keepdims=True)
        vals.append(m)
        idxs.append(idx)
        work = jnp.where(lane_f == idx, -jnp.inf, work)
    exps = [jnp.exp(v - vals[0]) for v in vals]
    tot = exps[0] + exps[1] + exps[2] + exps[3]
    ti = jnp.zeros((rows, LANES), F32)
    tp = jnp.zeros((rows, LANES), F32)
    for k in range(TOP_K):
        ti = jnp.where(lane == k, idxs[k], ti)
        tp = jnp.where(lane == k, exps[k] / tot, tp)
    ti_ref[...] = ti.astype(jnp.int32)
    tp_ref[...] = tp


def _mixer(x2, seq, attn, lw):
    t, d = x2.shape
    assert d == SUBLANES * LANES
    rows = MIX_ROWS
    tps = seq // rows
    assert seq % rows == 0 and rows % CHUNK == 0
    halo_per_tile = rows // HALO
    last_halo = t // HALO - 1
    tile = lambda w: pl.BlockSpec((rows, w), lambda i: (i, 0))

    def sub_tile(arr):
        _, dil, _, w = arr.shape
        assert (rows // dil) % 16 == 0
        return pl.BlockSpec((None, dil, rows // dil, w), lambda i: (i // tps, 0, i % tps, 0))

    in_specs = [tile(d),
                pl.BlockSpec((HALO, d), lambda i: (jnp.maximum(i * halo_per_tile - 1, 0), 0)),
                pl.BlockSpec((HALO, d), lambda i: (jnp.minimum((i + 1) * halo_per_tile, last_halo), 0))]
    in_specs += [sub_tile(a[0]) for a in attn] + [sub_tile(a[1]) for a in attn]
    weights = (lw["w_abc"], lw["spatial_w"], lw["spatial_b"], lw["conv_w"], lw["pool_w"], lw["pool_scale"],
               lw["w_branch"], lw["w_gate"], lw["w_out"], lw["ln1_g"], lw["ln1_b"],
               lw["router_hi"], lw["router_lo"], lw["router_b"])
    in_specs += [_resident(w.shape) for w in weights]
    interleave = [pltpu.VMEM((ATT_HEADS, rows, HEAD_DIM), F32), pltpu.VMEM((rows, LANES), F32)]
    return pl.pallas_call(
        functools.partial(_mix_kernel, rows=rows, tiles_per_seq=tps, seq=seq),
        grid=(t // rows,),
        in_specs=in_specs,
        out_specs=[pl.BlockSpec((rows, SUBLANES, LANES), lambda i: (i, 0, 0)), tile(LANES), tile(LANES)],
        out_shape=[jax.ShapeDtypeStruct((t, SUBLANES, LANES), F32),
                   jax.ShapeDtypeStruct((t, LANES), jnp.int32),
                   jax.ShapeDtypeStruct((t, LANES), F32)],
        scratch_shapes=[pltpu.VMEM((rows + 2 * HALO, d), BF16),
                        pltpu.VMEM((rows + 2 * HALO, B_WIDTH), F32),
                        pltpu.VMEM((rows + 2 * HALO, C_WIDTH), F32)] + interleave + interleave,
        compiler_params=_params(1),
        name="mixer",
    )(x2, x2, x2, attn[0][0], attn[1][0], attn[2][0], attn[0][1], attn[1][1], attn[2][1], *weights)


def _gather_copy(x_hbm, idx_ref, gbuf, sem, i):
    return pltpu.make_async_copy(x_hbm.at[pl.ds(idx_ref[0, 0, i], 1)], gbuf.at[pl.ds(i, 1)], sem)


def _scatter_copy(obuf, idx_ref, y_hbm, sem, i):
    return pltpu.make_async_copy(obuf.at[pl.ds(i, 1)], y_hbm.at[pl.ds(idx_ref[0, 0, i], 1)], sem)


def _expert_kernel(be_ref, nu_ref, tok_ref, dst_ref, x_hbm, wgu_ref, bgu_ref, wd_ref, bd_ref, y_hbm,
                   gbuf, obuf, gsem, osem, *, n_blocks):
    m = EXPERT_ROWS
    s = pl.program_id(0)
    n_used = nu_ref[0]
    b = s - 1

    def scatter_wait(blk):
        slot = blk % 2
        pltpu.make_async_copy(obuf.at[slot], y_hbm.at[pl.ds(0, m)], osem.at[slot]).wait()

    @pl.when(s == 0)
    def _():
        obuf[0] = jnp.zeros(obuf.shape[1:], F32)
        n_rows = y_hbm.shape[0]
        for j in range(SCATTER_DEPTH):
            dump = pltpu.make_async_copy(obuf.at[0], y_hbm.at[pl.ds(n_rows - (j + 1) * m, m)], osem.at[0])
            dump.start()
            dump.wait()

    @pl.when((b >= 2) & (b - 2 < n_used))
    def _():
        scatter_wait(b - 2)

    @pl.when((s < n_blocks) & (s < n_used))
    def _():
        for i in range(m):
            _gather_copy(x_hbm, tok_ref, gbuf.at[s % 2], gsem.at[s % 2], i).start()

    @pl.when((b >= 0) & (b < n_used))
    def _():
        slot = b % 2
        pltpu.make_async_copy(x_hbm.at[pl.ds(0, m)], gbuf.at[slot], gsem.at[slot]).wait()
        xg = gbuf[slot].reshape(m, SUBLANES * LANES).astype(BF16)
        h = _dot(xg, wgu_ref[...]) + bgu_ref[...]
        g = jnp.minimum(h[:, :D_FF], SWIGLU_LIMIT)
        u = jnp.clip(h[:, D_FF:], -SWIGLU_LIMIT, SWIGLU_LIMIT)
        act = g * jax.nn.sigmoid(SWIGLU_ALPHA * g) * (u + 1.0)
        y = _dot(act.astype(BF16), wd_ref[...]) + bd_ref[...]
        obuf[slot] = y.reshape(m, SUBLANES, LANES)
        for i in range(m):
            _scatter_copy(obuf.at[slot], dst_ref, y_hbm, osem.at[slot], i).start()

    @pl.when(s == n_blocks)
    def _():
        @pl.when((b >= 1) & (b - 1 < n_used))
        def _():
            scatter_wait(b - 1)

        @pl.when(b < n_used)
        def _():
            scatter_wait(b)


def _experts(x1, slot_tok, slot_dst, block_e, n_used, lw):
    t = x1.shape[0]
    d = D_MODEL
    nb = block_e.shape[0]
    m = EXPERT_ROWS
    cur = lambda s: jnp.maximum(s - 1, 0)
    grid_spec = pltpu.PrefetchScalarGridSpec(
        num_scalar_prefetch=2,
        grid=(nb + 1,),
        in_specs=[pl.BlockSpec((1, 1, m), lambda s, be, nu: (jnp.minimum(s, nb - 1), 0, 0), memory_space=pltpu.SMEM),
                  pl.BlockSpec((1, 1, m), lambda s, be, nu: (cur(s), 0, 0), memory_space=pltpu.SMEM),
                  pl.BlockSpec(memory_space=pl.ANY),
                  pl.BlockSpec((None, d, 2 * D_FF), lambda s, be, nu: (be[cur(s)], 0, 0)),
                  pl.BlockSpec((None, 1, 2 * D_FF), lambda s, be, nu: (be[cur(s)], 0, 0)),
                  pl.BlockSpec((None, D_FF, d), lambda s, be, nu: (be[cur(s)], 0, 0)),
                  pl.BlockSpec((None, 1, d), lambda s, be, nu: (be[cur(s)], 0, 0))],
        out_specs=pl.BlockSpec(memory_space=pl.ANY),
        scratch_shapes=[pltpu.VMEM((2, m, SUBLANES, LANES), F32), pltpu.VMEM((2, m, SUBLANES, LANES), F32),
                        pltpu.SemaphoreType.DMA((2,)), pltpu.SemaphoreType.DMA((2,))],
    )
    return pl.pallas_call(
        functools.partial(_expert_kernel, n_blocks=nb),
        grid_spec=grid_spec,
        out_shape=jax.ShapeDtypeStruct((TOP_K * t + SCATTER_DEPTH * m, SUBLANES, LANES), F32),
        compiler_params=_params(1),
        name="expert_blocks",
    )(block_e, n_used, slot_tok, slot_dst, x1, lw["w_gate_up"], lw["b_gate_up"], lw["w_down"], lw["b_down"])


def _combine_kernel(y0_ref, y1_ref, y2_ref, y3_ref, x1_ref, p_ref, tp_ref, wpg_ref, wpp_ref, g2_ref, b2_ref, o_ref):
    rows = x1_ref.shape[0]
    x1 = x1_ref[...].reshape(rows, SUBLANES * LANES)
    tp = tp_ref[...]
    moe = None
    for k, y_ref in enumerate((y0_ref, y1_ref, y2_ref, y3_ref)):
        term = tp[:, k:k + 1] * y_ref[...].reshape(rows, SUBLANES * LANES)
        moe = term if moe is None else moe + term
    ple = jax.nn.sigmoid(_dot(x1.astype(BF16), wpg_ref[...])) * _dot(p_ref[...].astype(BF16), wpp_ref[...])
    o_ref[...] = _layer_norm(ALPHA * x1 + moe + ple) * g2_ref[...] + b2_ref[...]


def _combine(x1, p2, top_p, y, lw):
    t = x1.shape[0]
    d = D_MODEL
    rows = COMBINE_ROWS
    n_tiles = t // rows
    tile = lambda w: pl.BlockSpec((rows, w), lambda i: (i, 0))
    choice = lambda k: pl.BlockSpec((rows, SUBLANES, LANES), lambda i: (k * n_tiles + i, 0, 0))
    weights = (lw["w_ple_gate"], lw["w_ple_proj"], lw["ln2_g"], lw["ln2_b"])
    return pl.pallas_call(
        _combine_kernel,
        grid=(n_tiles,),
        in_specs=[choice(k) for k in range(TOP_K)] + [choice(0), tile(PLE_DIM), tile(LANES)]
        + [_resident(w.shape) for w in weights],
        out_specs=tile(d),
        out_shape=jax.ShapeDtypeStruct((t, d), F32),
        compiler_params=_params(1),
        name="combine",
    )(y, y, y, y, x1, p2, top_p, *weights)


def _routing_tables(top_i):
    t = top_i.shape[0]
    a = t * TOP_K
    m = EXPERT_ROWS
    nb = -(-(a + N_EXPERTS * (m - 1)) // m)
    e_flat = top_i.reshape(a)
    keys = jnp.sort(e_flat * a + jnp.arange(a, dtype=jnp.int32))
    a_sorted = keys % a
    experts = jnp.arange(N_EXPERTS, dtype=jnp.int32)
    start_sorted = jnp.sum(keys[None, :] < (experts * a)[:, None], axis=1, dtype=jnp.int32)
    counts = jnp.concatenate([start_sorted[1:], jnp.full((1,), a, jnp.int32)]) - start_sorted
    padded = (counts + m - 1) // m * m
    ends_padded = jnp.cumsum(padded)
    start_padded = ends_padded - padded
    block_start = jnp.arange(nb, dtype=jnp.int32) * m
    block_e = jnp.minimum(jnp.sum(block_start[:, None] >= ends_padded[None, :], axis=1, dtype=jnp.int32),
                          N_EXPERTS - 1)
    n_used = (ends_padded[-1] // m).astype(jnp.int32).reshape(1)
    within = jnp.arange(m, dtype=jnp.int32)[None, :]
    rank = block_start[:, None] + within - start_padded[block_e][:, None]
    valid = (rank >= 0) & (rank < counts[block_e][:, None])
    src = a_sorted[jnp.clip(start_sorted[block_e][:, None] + rank, 0, a - 1)]
    slot_tok = jnp.where(valid, src // TOP_K, 0)
    dump = a + (jnp.arange(nb, dtype=jnp.int32) % SCATTER_DEPTH)[:, None] * m + within
    slot_dst = jnp.where(valid, (src % TOP_K) * t + src // TOP_K, dump)
    return slot_tok.reshape(nb, 1, m), slot_dst.reshape(nb, 1, m), block_e, n_used


def _layer(x, p, lw):
    bn, s, d = x.shape
    x2 = x.reshape(bn * s, d)
    attn = []
    for g, (_, dil) in enumerate(ATT_CONFIGS):
        qkv = _qkv_proj(x, lw["w_qkv"][g], dil)
        attn.append(_attention(qkv, g))
    x1, top_i, top_p = _mixer(x2, s, attn, lw)
    slot_tok, slot_dst, block_e, n_used = _routing_tables(top_i[:, :TOP_K])
    y = _experts(x1, slot_tok, slot_dst, block_e, n_used, lw)
    out = _combine(x1, p.reshape(bn * s, PLE_DIM), top_p, y, lw)
    return out.reshape(bn, s, d)


def _prepare_layer(i, w_in, spatial_w, spatial_b, conv_w, pool_w, pool_scale, w_branch, w_gate, w_out, ln1_g, ln1_b,
                   router_w, router_b, w_gate_up, b_gate_up, w_down, b_down, w_ple_gate, w_ple_proj, ln2_g, ln2_b):
    rw = jnp.pad(router_w[i], ((0, 0), (0, LANES - N_EXPERTS)))
    rw_hi = rw.astype(BF16)
    row = lambda v: v[i].reshape(1, -1)
    return {
        "w_abc": w_in[i][:, :N_ABC].astype(BF16),
        "w_qkv": [w_in[i][:, N_ABC + g * N_QKV:N_ABC + (g + 1) * N_QKV].astype(BF16) for g in range(len(ATT_CONFIGS))],
        "spatial_w": spatial_w[i].astype(BF16),
        "spatial_b": jnp.broadcast_to(spatial_b[i][:, :, None], (A_GROUPS, CHUNK, A_WIDTH // A_GROUPS)),
        "conv_w": conv_w[i],
        "pool_w": pool_w[i].astype(BF16),
        "pool_scale": row(pool_scale),
        "w_branch": w_branch[i].astype(BF16),
        "w_gate": w_gate[i].astype(BF16),
        "w_out": w_out[i].astype(BF16),
        "ln1_g": row(ln1_g), "ln1_b": row(ln1_b),
        "router_hi": rw_hi,
        "router_lo": (rw - rw_hi.astype(F32)).astype(BF16),
        "router_b": jnp.pad(router_b[i], (0, LANES - N_EXPERTS), constant_values=NEG_INF).reshape(1, LANES),
        "w_gate_up": w_gate_up[i].astype(BF16),
        "b_gate_up": b_gate_up[i].reshape(N_EXPERTS, 1, 2 * D_FF),
        "w_down": w_down[i].astype(BF16),
        "b_down": b_down[i].reshape(N_EXPERTS, 1, D_MODEL),
        "w_ple_gate": w_ple_gate[i].astype(BF16),
        "w_ple_proj": w_ple_proj[i].astype(BF16),
        "ln2_g": row(ln2_g), "ln2_b": row(ln2_b),
    }


def _trunk(x, p, layers):
    for i, lw in enumerate(layers):
        x = _layer(x, p[i], lw)
    return x


def kernel(x_prompt, x_sample, p_prompt, p_sample, w_in, spatial_w, spatial_b, conv_w, pool_w, pool_scale, w_branch,
           w_gate, w_out, ln1_g, ln1_b, router_w, router_b, w_gate_up, b_gate_up, w_down, b_down, w_ple_gate,
           w_ple_proj, ln2_g, ln2_b):
    weights = (w_in, spatial_w, spatial_b, conv_w, pool_w, pool_scale, w_branch, w_gate, w_out, ln1_g, ln1_b,
               router_w, router_b, w_gate_up, b_gate_up, w_down, b_down, w_ple_gate, w_ple_proj, ln2_g, ln2_b)
    layers = [_prepare_layer(i, *weights) for i in range(w_in.shape[0])]
    return _trunk(x_prompt, p_prompt, layers), _trunk(x_sample, p_sample, layers)
```

```python
import functools

import numpy as np
import jax
import jax.numpy as jnp
from jax import lax
from jax.experimental import pallas as pl
from jax.experimental.pallas import tpu as pltpu

D_MODEL = 1024
PLE_DIM = 256
CHUNK = 128
A_WIDTH = 512
A_GROUPS = 4
B_WIDTH = 512
C_WIDTH = 512
POOL_WINDOWS = (2, 4, 8, 16)
ATT_CONFIGS = ((128, 1), (512, 4), (2048, 16))
ATT_HEADS = 4
HEAD_DIM = 128
D_WIDTH = ATT_HEADS * HEAD_DIM
N_ABC = 2 * A_WIDTH + 3 * B_WIDTH + C_WIDTH
N_QKV = 3 * D_WIDTH
N_EXPERTS = 32
TOP_K = 4
D_FF = 1024
SWIGLU_LIMIT = 7.0
SWIGLU_ALPHA = 1.702
DEPTH = 2
ALPHA = (2 * DEPTH) ** 0.25
LN_EPS = 1e-5
NEG_INF = -1e30

LANES = 128
SUBLANES = 8
HALO = 8
RADIUS = 64
Q_BLOCK = 128
PROJ_ROWS = 1024
ATT_ROWS = 512
MIX_ROWS = 512
EXPERT_ROWS = 256
ROUTE_ROWS = 256
VMEM_LIMIT = 56 * 1024 * 1024

F32 = jnp.float32
BF16 = jnp.bfloat16


def _params(n_axes, vmem=VMEM_LIMIT):
    return pltpu.CompilerParams(dimension_semantics=("arbitrary",) * n_axes, vmem_limit_bytes=vmem)


def _resident(shape):
    zeros = (0,) * len(shape)
    return pl.BlockSpec(shape, lambda *_: zeros, pipeline_mode=pl.Buffered(1))


def _dot(a, b):
    return jnp.dot(a, b, preferred_element_type=F32)


def _layer_norm(x):
    mu = jnp.mean(x, axis=-1, keepdims=True)
    xc = x - mu
    var = jnp.mean(xc * xc, axis=-1, keepdims=True)
    return xc * lax.rsqrt(var + LN_EPS)


def _qkv_kernel(x_ref, w_ref, o_ref, *scratch, dil):
    rows, d = x_ref.shape
    if dil == 1:
        xe = x_ref[...].astype(BF16)
    else:
        lt_ref, xe_ref = scratch
        n = rows // dil
        for t in range(d // LANES):
            lt_ref[t] = x_ref[:, t * LANES:(t + 1) * LANES]
        for rho in range(dil):
            piece = jnp.concatenate([lt_ref[t, pl.ds(rho, n, stride=dil), :] for t in range(d // LANES)], axis=1)
            xe_ref[rho * n:(rho + 1) * n, :] = piece.astype(BF16)
        xe = xe_ref[...]
    y = _dot(xe, w_ref[...])
    o_ref[...] = y.astype(BF16).reshape(dil, rows // dil, N_QKV)


def _qkv_proj(x, w, dil):
    bn, s, d = x.shape
    rows = PROJ_ROWS
    assert s % rows == 0 and (rows // dil) % 16 == 0
    scratch = [] if dil == 1 else [pltpu.VMEM((d // LANES, rows, LANES), F32), pltpu.VMEM((rows, d), BF16)]
    return pl.pallas_call(
        functools.partial(_qkv_kernel, dil=dil),
        grid=(bn, s // rows),
        in_specs=[pl.BlockSpec((None, rows, d), lambda b, i: (b, i, 0)),
                  _resident((d, N_QKV))],
        out_specs=pl.BlockSpec((None, dil, rows // dil, N_QKV), lambda b, i: (b, 0, i, 0)),
        out_shape=jax.ShapeDtypeStruct((bn, dil, s // dil, N_QKV), BF16),
        scratch_shapes=scratch,
        compiler_params=_params(2),
        name=f"qkv_proj_d{dil}",
    )(x, w)


def _attn_kernel(main_ref, kp_ref, vp_ref, kn_ref, vn_ref, o_ref, lse_ref, kbuf, vbuf, *, sub, rows, dil, slopes):
    i = pl.program_id(2)
    kbuf[0:RADIUS, :] = kp_ref[...]
    kbuf[RADIUS:RADIUS + rows, :] = main_ref[:, D_WIDTH:2 * D_WIDTH]
    kbuf[RADIUS + rows:, :] = kn_ref[...]
    vbuf[0:RADIUS, :] = vp_ref[...]
    vbuf[RADIUS:RADIUS + rows, :] = main_ref[:, 2 * D_WIDTH:3 * D_WIDTH]
    vbuf[RADIUS + rows:, :] = vn_ref[...]

    window = Q_BLOCK + 2 * RADIUS
    row = lax.broadcasted_iota(jnp.int32, (Q_BLOCK, window), 0)
    col = lax.broadcasted_iota(jnp.int32, (Q_BLOCK, window), 1)
    dist = jnp.abs(col - RADIUS - row)
    in_band = dist <= RADIUS
    dist_f = (dist * dil).astype(F32)
    lane = lax.broadcasted_iota(jnp.int32, (Q_BLOCK, LANES), 1)
    scale = HEAD_DIM ** -0.5
    for j in range(rows // Q_BLOCK):
        kpos = i * rows + j * Q_BLOCK - RADIUS + col
        valid = in_band & (kpos >= 0) & (kpos < sub)
        lse_tile = jnp.zeros((Q_BLOCK, LANES), F32)
        for h in range(ATT_HEADS):
            hs = slice(h * HEAD_DIM, (h + 1) * HEAD_DIM)
            q = main_ref[j * Q_BLOCK:(j + 1) * Q_BLOCK, hs]
            k = kbuf[j * Q_BLOCK:j * Q_BLOCK + window, hs]
            v = vbuf[j * Q_BLOCK:j * Q_BLOCK + window, hs]
            s = lax.dot_general(q, k, (((1,), (1,)), ((), ())), preferred_element_type=F32)
            s = s * scale + (-slopes[h]) * dist_f
            s = jnp.where(valid, s, NEG_INF)
            m = jnp.max(s, axis=-1, keepdims=True)
            p = jnp.exp(s - m)
            l = jnp.sum(p, axis=-1, keepdims=True)
            o = _dot(p.astype(BF16), v) / l
            o_ref[j * Q_BLOCK:(j + 1) * Q_BLOCK, hs] = o.astype(BF16)
            lse_tile = jnp.where(lane == h, m + jnp.log(l), lse_tile)
        lse_ref[j * Q_BLOCK:(j + 1) * Q_BLOCK, :] = lse_tile


def _attention(qkv, group):
    bn, dil, sub, _ = qkv.shape
    rows = min(sub, ATT_ROWS)
    assert sub % rows == 0 and rows % Q_BLOCK == 0 and sub % RADIUS == 0
    n_groups = len(ATT_CONFIGS) * ATT_HEADS
    slopes = tuple(float(np.float32(2.0 ** (-8.0 * (group * ATT_HEADS + h + 1) / n_groups))) for h in range(ATT_HEADS))
    halo_per_tile = rows // RADIUS
    last_halo = sub // RADIUS - 1

    def prev_map(col):
        return lambda b, r, i: (b, r, jnp.maximum(i * halo_per_tile - 1, 0), col)

    def next_map(col):
        return lambda b, r, i: (b, r, jnp.minimum((i + 1) * halo_per_tile, last_halo), col)

    return pl.pallas_call(
        functools.partial(_attn_kernel, sub=sub, rows=rows, dil=dil, slopes=slopes),
        grid=(bn, dil, sub // rows),
        in_specs=[pl.BlockSpec((None, None, rows, N_QKV), lambda b, r, i: (b, r, i, 0)),
                  pl.BlockSpec((None, None, RADIUS, D_WIDTH), prev_map(1)),
                  pl.BlockSpec((None, None, RADIUS, D_WIDTH), prev_map(2)),
                  pl.BlockSpec((None, None, RADIUS, D_WIDTH), next_map(1)),
                  pl.BlockSpec((None, None, RADIUS, D_WIDTH), next_map(2))],
        out_specs=[pl.BlockSpec((None, None, rows, D_WIDTH), lambda b, r, i: (b, r, i, 0)),
                   pl.BlockSpec((None, None, rows, LANES), lambda b, r, i: (b, r, i, 0))],
        out_shape=[jax.ShapeDtypeStruct((bn, dil, sub, D_WIDTH), BF16),
                   jax.ShapeDtypeStruct((bn, dil, sub, LANES), F32)],
        scratch_shapes=[pltpu.VMEM((rows + 2 * RADIUS, D_WIDTH), BF16),
                        pltpu.VMEM((rows + 2 * RADIUS, D_WIDTH), BF16)],
        compiler_params=_params(3),
        name=f"band_attention_d{dil}",
    )(qkv, qkv, qkv, qkv, qkv)


def _mix_kernel(x_ref, xp_ref, xn_ref, o0_ref, o1_ref, o2_ref, l0_ref, l1_ref, l2_ref,
                win_ref, ws_ref, bs_ref, cw_ref, pw_ref, ps_ref, wbr_ref, wg_ref, wo_ref,
                g1_ref, b1_ref, rwh_ref, rwl_ref, rb_ref, tri_ref,
                x1_ref, ti_ref, tp_ref, cnt_ref,
                xe_ref, zb_ref, cb_ref, so1_ref, sl1_ref, so2_ref, sl2_ref, *, rows, tiles_per_seq, seq):
    tis = pl.program_id(0) % tiles_per_seq
    has_prev = tis > 0
    has_next = tis < tiles_per_seq - 1

    x = x_ref[...]
    xe_ref[0:rows, :] = x.astype(BF16)
    xe_ref[rows:rows + 2 * HALO, :] = jnp.concatenate([xp_ref[...], xn_ref[...]], axis=0).astype(BF16)
    xm = xe_ref[0:rows, :]

    h_a = _dot(xm, win_ref[:, 0:2 * A_WIDTH])
    a_u = h_a[:, 0:A_WIDTH]
    vn = _layer_norm(h_a[:, A_WIDTH:2 * A_WIDTH]).astype(BF16)
    gw = A_WIDTH // A_GROUPS
    chunks = []
    for c in range(rows // CHUNK):
        cols = [_dot(ws_ref[g], vn[c * CHUNK:(c + 1) * CHUNK, g * gw:(g + 1) * gw]) + bs_ref[g]
                for g in range(A_GROUPS)]
        chunks.append(jnp.concatenate(cols, axis=1))
    y_a = a_u * jnp.concatenate(chunks, axis=0)

    h_bc = _dot(xe_ref[...], win_ref[:, 2 * A_WIDTH:N_ABC])
    b_b = h_bc[0:rows, B_WIDTH:2 * B_WIDTH]
    z = h_bc[:, 2 * B_WIDTH:3 * B_WIDTH] * h_bc[:, 0:B_WIDTH]
    cz = h_bc[:, 3 * B_WIDTH:3 * B_WIDTH + C_WIDTH]
    for buf, val in ((zb_ref, z), (cb_ref, cz)):
        buf[0:HALO, :] = jnp.where(has_prev, val[rows:rows + HALO], 0.0)
        buf[HALO:HALO + rows, :] = val[0:rows]
        buf[HALO + rows:, :] = jnp.where(has_next, val[rows + HALO:], 0.0)

    conv = (cw_ref[0:1, :] * zb_ref[HALO - 1:HALO - 1 + rows, :]
            + cw_ref[1:2, :] * zb_ref[HALO:HALO + rows, :]
            + cw_ref[2:3, :] * zb_ref[HALO + 1:HALO + 1 + rows, :])
    y_b = b_b * conv

    pos = tis * rows + lax.broadcasted_iota(jnp.int32, (rows, 1), 0)
    cgw = C_WIDTH // len(POOL_WINDOWS)
    pooled_out = []
    for g, w in enumerate(POOL_WINDOWS):
        cs = slice(g * cgw, (g + 1) * cgw)
        acc = None
        for dd in range(-(w // 2), w // 2):
            v = cb_ref[HALO + dd:HALO + dd + rows, cs]
            acc = v if acc is None else acc + v
        count = (jnp.minimum(pos + w // 2, seq) - jnp.maximum(pos - w // 2, 0)).astype(F32)
        pooled = acc / count - cb_ref[HALO:HALO + rows, cs]
        pooled_out.append(_dot(pooled.astype(BF16), pw_ref[g]))
    y_c = jnp.concatenate(pooled_out, axis=1) * ps_ref[...]

    def token_order(o_ref, l_ref, so_ref, sl_ref):
        dil = o_ref.shape[0]
        if dil == 1:
            return [o_ref[0, :, h * HEAD_DIM:(h + 1) * HEAD_DIM].astype(F32) for h in range(ATT_HEADS)], l_ref[0]
        n = rows // dil
        for rho in range(dil):
            v = o_ref[rho].astype(F32)
            for h in range(ATT_HEADS):
                so_ref[h, pl.ds(rho, n, stride=dil), :] = v[:, h * HEAD_DIM:(h + 1) * HEAD_DIM]
            sl_ref[pl.ds(rho, n, stride=dil), :] = l_ref[rho]
        return [so_ref[h] for h in range(ATT_HEADS)], sl_ref[...]

    outs, lses = zip(token_order(o0_ref, l0_ref, None, None),
                     token_order(o1_ref, l1_ref, so1_ref, sl1_ref),
                     token_order(o2_ref, l2_ref, so2_ref, sl2_ref))
    top = jnp.maximum(jnp.maximum(lses[0], lses[1]), lses[2])
    es = [jnp.exp(l - top) for l in lses]
    den = es[0] + es[1] + es[2]
    wts = [e / den for e in es]
    heads = []
    for h in range(ATT_HEADS):
        acc = None
        for g in range(len(ATT_CONFIGS)):
            term = wts[g][:, h:h + 1] * outs[g][h]
            acc = term if acc is None else acc + term
        heads.append(acc)
    y_d = jnp.concatenate(heads, axis=1)

    merged = None
    for n, y in enumerate((y_a, y_b, y_c, y_d)):
        term = jax.nn.sigmoid(_dot(xm, wg_ref[n])) * _dot(y.astype(BF16), wbr_ref[n])
        merged = term if merged is None else merged + term
    u = ALPHA * x + _dot(merged.astype(BF16), wo_ref[...])
    x1 = _layer_norm(u) * g1_ref[...] + b1_ref[...]
    x1_ref[...] = x1.reshape(rows, SUBLANES, LANES)

    x1h = x1.astype(BF16)
    x1l = (x1 - x1h.astype(F32)).astype(BF16)
    logits = _dot(x1h, rwh_ref[...]) + _dot(x1l, rwh_ref[...]) + _dot(x1h, rwl_ref[...]) + rb_ref[...]
    lane = lax.broadcasted_iota(jnp.int32, (rows, LANES), 1)
    lane_f = lane.astype(F32)
    vals, idxs = [], []
    work = logits
    for _ in range(TOP_K):
        m = jnp.max(work, axis=-1, keepdims=True)
        idx = jnp.min(jnp.where(work == m, lane_f, float(LANES)), axis=-1, keepdims=True)
        vals.append(m)
        idxs.append(idx)
        work = jnp.where(lane_f == idx, -jnp.inf, work)
    exps = [jnp.exp(v - vals[0]) for v in vals]
    tot = exps[0] + exps[1] + exps[2] + exps[3]
    ti = jnp.zeros((rows, LANES), F32)
    tp = jnp.zeros((rows, LANES), F32)
    for k in range(TOP_K):
        ti = jnp.where(lane == k, idxs[k], ti)
        tp = jnp.where(lane == k, exps[k] / tot, tp)
    hot = jnp.zeros((rows, LANES), F32)
    for k in range(TOP_K):
        hot = hot + jnp.where(lane_f == idxs[k], 1.0, 0.0)
    earlier = _dot(tri_ref[...], hot.astype(BF16))
    for k in range(TOP_K):
        rank = jnp.sum(jnp.where(lane_f == idxs[k], earlier, 0.0), axis=-1, keepdims=True)
        ti = jnp.where(lane == TOP_K + k, rank, ti)
    ti_ref[...] = ti.astype(jnp.int32)
    tp_ref[...] = tp
    cnt_ref[...] = jnp.broadcast_to(jnp.sum(hot, axis=0, keepdims=True), (SUBLANES, LANES)).astype(jnp.int32)


def _mixer(x2, seq, attn, lw):
    t, d = x2.shape
    assert d == SUBLANES * LANES
    rows = MIX_ROWS
    tps = seq // rows
    assert seq % rows == 0 and rows % CHUNK == 0
    halo_per_tile = rows // HALO
    last_halo = t // HALO - 1
    tile = lambda w: pl.BlockSpec((rows, w), lambda i: (i, 0))

    def sub_tile(arr):
        _, dil, _, w = arr.shape
        assert (rows // dil) % 16 == 0
        return pl.BlockSpec((None, dil, rows // dil, w), lambda i: (i // tps, 0, i % tps, 0))

    in_specs = [tile(d),
                pl.BlockSpec((HALO, d), lambda i: (jnp.maximum(i * halo_per_tile - 1, 0), 0)),
                pl.BlockSpec((HALO, d), lambda i: (jnp.minimum((i + 1) * halo_per_tile, last_halo), 0))]
    in_specs += [sub_tile(a[0]) for a in attn] + [sub_tile(a[1]) for a in attn]
    weights = (lw["w_abc"], lw["spatial_w"], lw["spatial_b"], lw["conv_w"], lw["pool_w"], lw["pool_scale"],
               lw["w_branch"], lw["w_gate"], lw["w_out"], lw["ln1_g"], lw["ln1_b"],
               lw["router_hi"], lw["router_lo"], lw["router_b"],
               jnp.tril(jnp.ones((rows, rows), BF16), k=-1))
    in_specs += [_resident(w.shape) for w in weights]
    interleave = [pltpu.VMEM((ATT_HEADS, rows, HEAD_DIM), F32), pltpu.VMEM((rows, LANES), F32)]
    return pl.pallas_call(
        functools.partial(_mix_kernel, rows=rows, tiles_per_seq=tps, seq=seq),
        grid=(t // rows,),
        in_specs=in_specs,
        out_specs=[pl.BlockSpec((rows, SUBLANES, LANES), lambda i: (i, 0, 0)), tile(LANES), tile(LANES),
                   pl.BlockSpec((None, SUBLANES, LANES), lambda i: (i, 0, 0))],
        out_shape=[jax.ShapeDtypeStruct((t, SUBLANES, LANES), F32),
                   jax.ShapeDtypeStruct((t, LANES), jnp.int32),
                   jax.ShapeDtypeStruct((t, LANES), F32),
                   jax.ShapeDtypeStruct((t // rows, SUBLANES, LANES), jnp.int32)],
        scratch_shapes=[pltpu.VMEM((rows + 2 * HALO, d), BF16),
                        pltpu.VMEM((rows + 2 * HALO, B_WIDTH), F32),
                        pltpu.VMEM((rows + 2 * HALO, C_WIDTH), F32)] + interleave + interleave,
        compiler_params=_params(1),
        name="mixer",
    )(x2, x2, x2, attn[0][0], attn[1][0], attn[2][0], attn[0][1], attn[1][1], attn[2][1], *weights)


def _dispatch_kernel(pos_ref, pad_ref, x_ref, xs_hbm, sbuf, zbuf, ssem, psem, *, n_tiles, rows, n_pad):
    i = pl.program_id(0)
    slot = i % 2

    def wait_rows(buf, sem):
        pltpu.make_async_copy(buf, xs_hbm.at[pl.ds(0, rows)], sem).wait()

    def wait_tile(sl):
        for _ in range(TOP_K):
            wait_rows(sbuf.at[sl], ssem.at[sl])

    @pl.when(i >= 2)
    def _():
        wait_tile(slot)

    sbuf[slot] = x_ref[...]
    for j in range(TOP_K * rows):
        pltpu.make_async_copy(sbuf.at[slot, pl.ds(j % rows, 1)], xs_hbm.at[pl.ds(pos_ref[0, 0, j], 1)],
                              ssem.at[slot]).start()

    @pl.when(i == n_tiles - 1)
    def _():
        zbuf[...] = jnp.zeros(zbuf.shape, F32)

        def fill(j, carry):
            pltpu.make_async_copy(zbuf, xs_hbm.at[pl.ds(pad_ref[0, 0, j], 1)], psem).start()
            return carry
        lax.fori_loop(0, n_pad, fill, 0, unroll=8)
        for _ in range(n_pad // rows):
            wait_rows(sbuf.at[0], psem)
        if n_tiles >= 2:
            wait_tile(1 - slot)
        wait_tile(slot)


def _dispatch(x1, pos, pad_slots, n_slots):
    t = x1.shape[0]
    rows = ROUTE_ROWS
    n_tiles = t // rows
    n_pad = pad_slots.shape[-1]
    assert t % rows == 0 and n_pad % rows == 0 and n_pad == n_slots - TOP_K * t
    return pl.pallas_call(
        functools.partial(_dispatch_kernel, n_tiles=n_tiles, rows=rows, n_pad=n_pad),
        grid=(n_tiles,),
        in_specs=[pl.BlockSpec((1, 1, TOP_K * rows), lambda i: (i, 0, 0), memory_space=pltpu.SMEM),
                  pl.BlockSpec((1, 1, n_pad), lambda i: (0, 0, 0), memory_space=pltpu.SMEM),
                  pl.BlockSpec((rows, SUBLANES, LANES), lambda i: (i, 0, 0))],
        out_specs=pl.BlockSpec(memory_space=pl.ANY),
        out_shape=jax.ShapeDtypeStruct((n_slots, SUBLANES, LANES), F32),
        scratch_shapes=[pltpu.VMEM((2, rows, SUBLANES, LANES), F32), pltpu.VMEM((1, SUBLANES, LANES), F32),
                        pltpu.SemaphoreType.DMA((2,)), pltpu.SemaphoreType.DMA(())],
        compiler_params=_params(1),
        name="dispatch",
    )(pos, pad_slots, x1)


def _expert_kernel(be_ref, nu_ref, xs_ref, wgu_ref, bgu_ref, wd_ref, bd_ref, y_ref):
    m = xs_ref.shape[0]
    b = pl.program_id(0)

    @pl.when(b < nu_ref[0])
    def _():
        xg = xs_ref[...].reshape(m, SUBLANES * LANES).astype(BF16)
        h = _dot(xg, wgu_ref[...]) + bgu_ref[...]
        g = jnp.minimum(h[:, :D_FF], SWIGLU_LIMIT)
        u = jnp.clip(h[:, D_FF:], -SWIGLU_LIMIT, SWIGLU_LIMIT)
        act = g * jax.nn.sigmoid(SWIGLU_ALPHA * g) * (u + 1.0)
        y = _dot(act.astype(BF16), wd_ref[...]) + bd_ref[...]
        y_ref[...] = y.reshape(m, SUBLANES, LANES)

    @pl.when(b >= nu_ref[0])
    def _():
        y_ref[...] = jnp.zeros(y_ref.shape, F32)


def _experts(xs, block_e, n_used, lw):
    m = EXPERT_ROWS
    d = D_MODEL
    nb = block_e.shape[0]
    rows_spec = pl.BlockSpec((m, SUBLANES, LANES), lambda b, be, nu: (b, 0, 0))
    grid_spec = pltpu.PrefetchScalarGridSpec(
        num_scalar_prefetch=2,
        grid=(nb,),
        in_specs=[rows_spec,
                  pl.BlockSpec((None, d, 2 * D_FF), lambda b, be, nu: (be[b], 0, 0)),
                  pl.BlockSpec((None, 1, 2 * D_FF), lambda b, be, nu: (be[b], 0, 0)),
                  pl.BlockSpec((None, D_FF, d), lambda b, be, nu: (be[b], 0, 0)),
                  pl.BlockSpec((None, 1, d), lambda b, be, nu: (be[b], 0, 0))],
        out_specs=rows_spec,
    )
    return pl.pallas_call(
        _expert_kernel,
        grid_spec=grid_spec,
        out_shape=jax.ShapeDtypeStruct(xs.shape, F32),
        compiler_params=_params(1),
        name="expert_blocks",
    )(block_e, n_used, xs, lw["w_gate_up"], lw["b_gate_up"], lw["w_down"], lw["b_down"])


def _combine_kernel(pos_ref, y_hbm, x1_ref, p_ref, tp_ref, wpg_ref, wpp_ref, g2_ref, b2_ref, o_ref, cbuf, csem,
                    *, n_tiles, rows):
    s = pl.program_id(0)
    n = TOP_K * rows

    @pl.when(s < n_tiles)
    def _():
        for j in range(n):
            pltpu.make_async_copy(y_hbm.at[pl.ds(pos_ref[0, 0, j], 1)], cbuf.at[s % 2, pl.ds(j, 1)],
                                  csem.at[s % 2]).start()

    @pl.when(s >= 1)
    def _():
        slot = (s - 1) % 2
        pltpu.make_async_copy(y_hbm.at[pl.ds(0, n)], cbuf.at[slot], csem.at[slot]).wait()
        x1 = x1_ref[...].reshape(rows, SUBLANES * LANES)
        tp = tp_ref[...]
        moe = None
        for k in range(TOP_K):
            term = tp[:, k:k + 1] * cbuf[slot, k * rows:(k + 1) * rows].reshape(rows, SUBLANES * LANES)
            moe = term if moe is None else moe + term
        ple = jax.nn.sigmoid(_dot(x1.astype(BF16), wpg_ref[...])) * _dot(p_ref[...].astype(BF16), wpp_ref[...])
        o_ref[...] = _layer_norm(ALPHA * x1 + moe + ple) * g2_ref[...] + b2_ref[...]


def _combine(x1, p2, top_p, pos, y, lw):
    t = x1.shape[0]
    d = D_MODEL
    rows = ROUTE_ROWS
    n_tiles = t // rows
    cur = lambda s: jnp.maximum(s - 1, 0)
    tile = lambda w: pl.BlockSpec((rows, w), lambda s: (cur(s), 0))
    weights = (lw["w_ple_gate"], lw["w_ple_proj"], lw["ln2_g"], lw["ln2_b"])
    return pl.pallas_call(
        functools.partial(_combine_kernel, n_tiles=n_tiles, rows=rows),
        grid=(n_tiles + 1,),
        in_specs=[pl.BlockSpec((1, 1, TOP_K * rows), lambda s: (jnp.minimum(s, n_tiles - 1), 0, 0),
                               memory_space=pltpu.SMEM),
                  pl.BlockSpec(memory_space=pl.ANY),
                  pl.BlockSpec((rows, SUBLANES, LANES), lambda s: (cur(s), 0, 0)), tile(PLE_DIM), tile(LANES)]
        + [_resident(w.shape) for w in weights],
        out_specs=tile(d),
        out_shape=jax.ShapeDtypeStruct((t, d), F32),
        scratch_shapes=[pltpu.VMEM((2, TOP_K * rows, SUBLANES, LANES), F32), pltpu.SemaphoreType.DMA((2,))],
        compiler_params=_params(1),
        name="combine",
    )(pos, y, x1, p2, top_p, *weights)


def _routing_tables(top_i, tile_counts):
    t = top_i.shape[0]
    a = t * TOP_K
    m = EXPERT_ROWS
    nb = -(-(a + N_EXPERTS * (m - 1)) // m)
    n_slots = nb * m
    n_pad = n_slots - a
    per_tile = tile_counts[:, 0, :N_EXPERTS]
    tile_base = jnp.cumsum(per_tile, axis=0) - per_tile
    counts = jnp.sum(per_tile, axis=0)
    padded = (counts + m - 1) // m * m
    ends_padded = jnp.cumsum(padded)
    start_padded = ends_padded - padded
    block_start = jnp.arange(nb, dtype=jnp.int32) * m
    block_e = jnp.minimum(jnp.sum(block_start[:, None] >= ends_padded[None, :], axis=1, dtype=jnp.int32),
                          N_EXPERTS - 1)
    n_used = (ends_padded[-1] // m).astype(jnp.int32).reshape(1)

    experts = top_i[:, :TOP_K]
    rank = top_i[:, TOP_K:2 * TOP_K]
    base = (start_padded[None, :] + tile_base).reshape(-1)
    tile = (jnp.arange(t, dtype=jnp.int32) // MIX_ROWS)[:, None]
    dest = base[tile * N_EXPERTS + experts] + rank
    r = ROUTE_ROWS
    pos = dest.reshape(t // r, r, TOP_K).transpose(0, 2, 1).reshape(t // r, 1, TOP_K * r)

    per = m - 1
    j = jnp.arange(per, dtype=jnp.int32)[None, :]
    in_tail = (j < (padded - counts)[:, None]).reshape(-1)
    tail_slot = ((start_padded + counts)[:, None] + j).reshape(-1)
    spare = ends_padded[-1] + jnp.cumsum(jnp.logical_not(in_tail).astype(jnp.int32)) - 1
    table = jnp.where(in_tail, tail_slot, spare)
    n_spare = N_EXPERTS * per - jnp.sum(padded - counts)
    rest = ends_padded[-1] + n_spare + jnp.arange(n_pad - N_EXPERTS * per, dtype=jnp.int32)
    pad_slots = jnp.concatenate([table, rest]).astype(jnp.int32).reshape(1, 1, n_pad)
    return pos, pad_slots, block_e, n_used, n_slots


def _layer(x, p, lw):
    bn, s, d = x.shape
    x2 = x.reshape(bn * s, d)
    attn = []
    for g, (_, dil) in enumerate(ATT_CONFIGS):
        qkv = _qkv_proj(x, lw["w_qkv"][g], dil)
        attn.append(_attention(qkv, g))
    x1, top_i, top_p, tile_counts = _mixer(x2, s, attn, lw)
    pos, pad_slots, block_e, n_used, n_slots = _routing_tables(top_i, tile_counts)
    xs = _dispatch(x1, pos, pad_slots, n_slots)
    y = _experts(xs, block_e, n_used, lw)
    out = _combine(x1, p.reshape(bn * s, PLE_DIM), top_p, pos, y, lw)
    return out.reshape(bn, s, d)


def _prepare_layer(i, w_in, spatial_w, spatial_b, conv_w, pool_w, pool_scale, w_branch, w_gate, w_out, ln1_g, ln1_b,
                   router_w, router_b, w_gate_up, b_gate_up, w_down, b_down, w_ple_gate, w_ple_proj, ln2_g, ln2_b):
    rw = jnp.pad(router_w[i], ((0, 0), (0, LANES - N_EXPERTS)))
    rw_hi = rw.astype(BF16)
    row = lambda v: v[i].reshape(1, -1)
    return {
        "w_abc": w_in[i][:, :N_ABC].astype(BF16),
        "w_qkv": [w_in[i][:, N_ABC + g * N_QKV:N_ABC + (g + 1) * N_QKV].astype(BF16) for g in range(len(ATT_CONFIGS))],
        "spatial_w": spatial_w[i].astype(BF16),
        "spatial_b": jnp.broadcast_to(spatial_b[i][:, :, None], (A_GROUPS, CHUNK, A_WIDTH // A_GROUPS)),
        "conv_w": conv_w[i],
        "pool_w": pool_w[i].astype(BF16),
        "pool_scale": row(pool_scale),
        "w_branch": w_branch[i].astype(BF16),
        "w_gate": w_gate[i].astype(BF16),
        "w_out": w_out[i].astype(BF16),
        "ln1_g": row(ln1_g), "ln1_b": row(ln1_b),
        "router_hi": rw_hi,
        "router_lo": (rw - rw_hi.astype(F32)).astype(BF16),
        "router_b": jnp.pad(router_b[i], (0, LANES - N_EXPERTS), constant_values=NEG_INF).reshape(1, LANES),
        "w_gate_up": w_gate_up[i].astype(BF16),
        "b_gate_up": b_gate_up[i].reshape(N_EXPERTS, 1, 2 * D_FF),
        "w_down": w_down[i].astype(BF16),
        "b_down": b_down[i].reshape(N_EXPERTS, 1, D_MODEL),
        "w_ple_gate": w_ple_gate[i].astype(BF16),
        "w_ple_proj": w_ple_proj[i].astype(BF16),
        "ln2_g": row(ln2_g), "ln2_b": row(ln2_b),
    }


def _trunk(x, p, layers):
    for i, lw in enumerate(layers):
        x = _layer(x, p[i], lw)
    return x


def kernel(x_prompt, x_sample, p_prompt, p_sample, w_in, spatial_w, spatial_b, conv_w, pool_w, pool_scale, w_branch,
           w_gate, w_out, ln1_g, ln1_b, router_w, router_b, w_gate_up, b_gate_up, w_down, b_down, w_ple_gate,
           w_ple_proj, ln2_g, ln2_b):
    weights = (w_in, spatial_w, spatial_b, conv_w, pool_w, pool_scale, w_branch, w_gate, w_out, ln1_g, ln1_b,
               router_w, router_b, w_gate_up, b_gate_up, w_down, b_down, w_ple_gate, w_ple_proj, ln2_g, ln2_b)
    layers = [_prepare_layer(i, *weights) for i in range(w_in.shape[0])]
    return _trunk(x_prompt, p_prompt, layers), _trunk(x_sample, p_sample, layers)
```

```python
import functools

import numpy as np
import jax
import jax.numpy as jnp
from jax import lax
from jax.experimental import pallas as pl
from jax.experimental.pallas import tpu as pltpu
from jax.experimental.pallas import tpu_sc as plsc

D_MODEL = 1024
PLE_DIM = 256
CHUNK = 128
A_WIDTH = 512
A_GROUPS = 4
B_WIDTH = 512
C_WIDTH = 512
POOL_WINDOWS = (2, 4, 8, 16)
ATT_CONFIGS = ((128, 1), (512, 4), (2048, 16))
ATT_HEADS = 4
HEAD_DIM = 128
D_WIDTH = ATT_HEADS * HEAD_DIM
N_ABC = 2 * A_WIDTH + 3 * B_WIDTH + C_WIDTH
N_QKV = 3 * D_WIDTH
N_EXPERTS = 32
TOP_K = 4
D_FF = 1024
SWIGLU_LIMIT = 7.0
SWIGLU_ALPHA = 1.702
DEPTH = 2
ALPHA = (2 * DEPTH) ** 0.25
LN_EPS = 1e-5
NEG_INF = -1e30

LANES = 128
SUBLANES = 8
HALO = 8
RADIUS = 64
Q_BLOCK = 128
PROJ_ROWS = 1024
ATT_ROWS = 512
MIX_ROWS = 512
EXPERT_ROWS = 256
COMBINE_ROWS = 512
SC_CORES = 2
SC_SUBCORES = 16
SC_CHUNK = 32
VMEM_LIMIT = 56 * 1024 * 1024

F32 = jnp.float32
BF16 = jnp.bfloat16


def _params(n_axes, vmem=VMEM_LIMIT):
    return pltpu.CompilerParams(dimension_semantics=("arbitrary",) * n_axes, vmem_limit_bytes=vmem)


def _resident(shape):
    zeros = (0,) * len(shape)
    return pl.BlockSpec(shape, lambda *_: zeros, pipeline_mode=pl.Buffered(1))


def _dot(a, b):
    return jnp.dot(a, b, preferred_element_type=F32)


def _layer_norm(x):
    mu = jnp.mean(x, axis=-1, keepdims=True)
    xc = x - mu
    var = jnp.mean(xc * xc, axis=-1, keepdims=True)
    return xc * lax.rsqrt(var + LN_EPS)


def _qkv_kernel(x_ref, w_ref, o_ref, *scratch, dil):
    rows, d = x_ref.shape
    if dil == 1:
        xe = x_ref[...].astype(BF16)
    else:
        lt_ref, xe_ref = scratch
        n = rows // dil
        for t in range(d // LANES):
            lt_ref[t] = x_ref[:, t * LANES:(t + 1) * LANES]
        for rho in range(dil):
            piece = jnp.concatenate([lt_ref[t, pl.ds(rho, n, stride=dil), :] for t in range(d // LANES)], axis=1)
            xe_ref[rho * n:(rho + 1) * n, :] = piece.astype(BF16)
        xe = xe_ref[...]
    y = _dot(xe, w_ref[...])
    o_ref[...] = y.astype(BF16).reshape(dil, rows // dil, N_QKV)


def _qkv_proj(x, w, dil):
    bn, s, d = x.shape
    rows = PROJ_ROWS
    assert s % rows == 0 and (rows // dil) % 16 == 0
    scratch = [] if dil == 1 else [pltpu.VMEM((d // LANES, rows, LANES), F32), pltpu.VMEM((rows, d), BF16)]
    return pl.pallas_call(
        functools.partial(_qkv_kernel, dil=dil),
        grid=(bn, s // rows),
        in_specs=[pl.BlockSpec((None, rows, d), lambda b, i: (b, i, 0)),
                  _resident((d, N_QKV))],
        out_specs=pl.BlockSpec((None, dil, rows // dil, N_QKV), lambda b, i: (b, 0, i, 0)),
        out_shape=jax.ShapeDtypeStruct((bn, dil, s // dil, N_QKV), BF16),
        scratch_shapes=scratch,
        compiler_params=_params(2),
        name=f"qkv_proj_d{dil}",
    )(x, w)


def _attn_kernel(main_ref, kp_ref, vp_ref, kn_ref, vn_ref, o_ref, lse_ref, kbuf, vbuf, *, sub, rows, dil, slopes):
    i = pl.program_id(2)
    kbuf[0:RADIUS, :] = kp_ref[...]
    kbuf[RADIUS:RADIUS + rows, :] = main_ref[:, D_WIDTH:2 * D_WIDTH]
    kbuf[RADIUS + rows:, :] = kn_ref[...]
    vbuf[0:RADIUS, :] = vp_ref[...]
    vbuf[RADIUS:RADIUS + rows, :] = main_ref[:, 2 * D_WIDTH:3 * D_WIDTH]
    vbuf[RADIUS + rows:, :] = vn_ref[...]

    window = Q_BLOCK + 2 * RADIUS
    row = lax.broadcasted_iota(jnp.int32, (Q_BLOCK, window), 0)
    col = lax.broadcasted_iota(jnp.int32, (Q_BLOCK, window), 1)
    dist = jnp.abs(col - RADIUS - row)
    in_band = dist <= RADIUS
    dist_f = (dist * dil).astype(F32)
    lane = lax.broadcasted_iota(jnp.int32, (Q_BLOCK, LANES), 1)
    scale = HEAD_DIM ** -0.5
    for j in range(rows // Q_BLOCK):
        kpos = i * rows + j * Q_BLOCK - RADIUS + col
        valid = in_band & (kpos >= 0) & (kpos < sub)
        lse_tile = jnp.zeros((Q_BLOCK, LANES), F32)
        for h in range(ATT_HEADS):
            hs = slice(h * HEAD_DIM, (h + 1) * HEAD_DIM)
            q = main_ref[j * Q_BLOCK:(j + 1) * Q_BLOCK, hs]
            k = kbuf[j * Q_BLOCK:j * Q_BLOCK + window, hs]
            v = vbuf[j * Q_BLOCK:j * Q_BLOCK + window, hs]
            s = lax.dot_general(q, k, (((1,), (1,)), ((), ())), preferred_element_type=F32)
            s = s * scale + (-slopes[h]) * dist_f
            s = jnp.where(valid, s, NEG_INF)
            m = jnp.max(s, axis=-1, keepdims=True)
            p = jnp.exp(s - m)
            l = jnp.sum(p, axis=-1, keepdims=True)
            o = _dot(p.astype(BF16), v) / l
            o_ref[j * Q_BLOCK:(j + 1) * Q_BLOCK, hs] = o.astype(BF16)
            lse_tile = jnp.where(lane == h, m + jnp.log(l), lse_tile)
        lse_ref[j * Q_BLOCK:(j + 1) * Q_BLOCK, :] = lse_tile


def _attention(qkv, group):
    bn, dil, sub, _ = qkv.shape
    rows = min(sub, ATT_ROWS)
    assert sub % rows == 0 and rows % Q_BLOCK == 0 and sub % RADIUS == 0
    n_groups = len(ATT_CONFIGS) * ATT_HEADS
    slopes = tuple(float(np.float32(2.0 ** (-8.0 * (group * ATT_HEADS + h + 1) / n_groups))) for h in range(ATT_HEADS))
    halo_per_tile = rows // RADIUS
    last_halo = sub // RADIUS - 1

    def prev_map(col):
        return lambda b, r, i: (b, r, jnp.maximum(i * halo_per_tile - 1, 0), col)

    def next_map(col):
        return lambda b, r, i: (b, r, jnp.minimum((i + 1) * halo_per_tile, last_halo), col)

    return pl.pallas_call(
        functools.partial(_attn_kernel, sub=sub, rows=rows, dil=dil, slopes=slopes),
        grid=(bn, dil, sub // rows),
        in_specs=[pl.BlockSpec((None, None, rows, N_QKV), lambda b, r, i: (b, r, i, 0)),
                  pl.BlockSpec((None, None, RADIUS, D_WIDTH), prev_map(1)),
                  pl.BlockSpec((None, None, RADIUS, D_WIDTH), prev_map(2)),
                  pl.BlockSpec((None, None, RADIUS, D_WIDTH), next_map(1)),
                  pl.BlockSpec((None, None, RADIUS, D_WIDTH), next_map(2))],
        out_specs=[pl.BlockSpec((None, None, rows, D_WIDTH), lambda b, r, i: (b, r, i, 0)),
                   pl.BlockSpec((None, None, rows, LANES), lambda b, r, i: (b, r, i, 0))],
        out_shape=[jax.ShapeDtypeStruct((bn, dil, sub, D_WIDTH), BF16),
                   jax.ShapeDtypeStruct((bn, dil, sub, LANES), F32)],
        scratch_shapes=[pltpu.VMEM((rows + 2 * RADIUS, D_WIDTH), BF16),
                        pltpu.VMEM((rows + 2 * RADIUS, D_WIDTH), BF16)],
        compiler_params=_params(3),
        name=f"band_attention_d{dil}",
    )(qkv, qkv, qkv, qkv, qkv)


def _mix_kernel(x_ref, xp_ref, xn_ref, o0_ref, o1_ref, o2_ref, l0_ref, l1_ref, l2_ref,
                win_ref, ws_ref, bs_ref, cw_ref, pw_ref, ps_ref, wbr_ref, wg_ref, wo_ref,
                g1_ref, b1_ref, rwh_ref, rwl_ref, rb_ref, tri_ref,
                x1_ref, ti_ref, tp_ref, cnt_ref,
                xe_ref, zb_ref, cb_ref, so1_ref, sl1_ref, so2_ref, sl2_ref, *, rows, tiles_per_seq, seq):
    tis = pl.program_id(0) % tiles_per_seq
    has_prev = tis > 0
    has_next = tis < tiles_per_seq - 1

    x = x_ref[...]
    xe_ref[0:rows, :] = x.astype(BF16)
    xe_ref[rows:rows + 2 * HALO, :] = jnp.concatenate([xp_ref[...], xn_ref[...]], axis=0).astype(BF16)
    xm = xe_ref[0:rows, :]

    h_a = _dot(xm, win_ref[:, 0:2 * A_WIDTH])
    a_u = h_a[:, 0:A_WIDTH]
    vn = _layer_norm(h_a[:, A_WIDTH:2 * A_WIDTH]).astype(BF16)
    gw = A_WIDTH // A_GROUPS
    chunks = []
    for c in range(rows // CHUNK):
        cols = [_dot(ws_ref[g], vn[c * CHUNK:(c + 1) * CHUNK, g * gw:(g + 1) * gw]) + bs_ref[g]
                for g in range(A_GROUPS)]
        chunks.append(jnp.concatenate(cols, axis=1))
    y_a = a_u * jnp.concatenate(chunks, axis=0)

    h_bc = _dot(xe_ref[...], win_ref[:, 2 * A_WIDTH:N_ABC])
    b_b = h_bc[0:rows, B_WIDTH:2 * B_WIDTH]
    z = h_bc[:, 2 * B_WIDTH:3 * B_WIDTH] * h_bc[:, 0:B_WIDTH]
    cz = h_bc[:, 3 * B_WIDTH:3 * B_WIDTH + C_WIDTH]
    for buf, val in ((zb_ref, z), (cb_ref, cz)):
        buf[0:HALO, :] = jnp.where(has_prev, val[rows:rows + HALO], 0.0)
        buf[HALO:HALO + rows, :] = val[0:rows]
        buf[HALO + rows:, :] = jnp.where(has_next, val[rows + HALO:], 0.0)

    conv = (cw_ref[0:1, :] * zb_ref[HALO - 1:HALO - 1 + rows, :]
            + cw_ref[1:2, :] * zb_ref[HALO:HALO + rows, :]
            + cw_ref[2:3, :] * zb_ref[HALO + 1:HALO + 1 + rows, :])
    y_b = b_b * conv

    pos = tis * rows + lax.broadcasted_iota(jnp.int32, (rows, 1), 0)
    cgw = C_WIDTH // len(POOL_WINDOWS)
    pooled_out = []
    for g, w in enumerate(POOL_WINDOWS):
        cs = slice(g * cgw, (g + 1) * cgw)
        acc = None
        for dd in range(-(w // 2), w // 2):
            v = cb_ref[HALO + dd:HALO + dd + rows, cs]
            acc = v if acc is None else acc + v
        count = (jnp.minimum(pos + w // 2, seq) - jnp.maximum(pos - w // 2, 0)).astype(F32)
        pooled = acc / count - cb_ref[HALO:HALO + rows, cs]
        pooled_out.append(_dot(pooled.astype(BF16), pw_ref[g]))
    y_c = jnp.concatenate(pooled_out, axis=1) * ps_ref[...]

    def token_order(o_ref, l_ref, so_ref, sl_ref):
        dil = o_ref.shape[0]
        if dil == 1:
            return [o_ref[0, :, h * HEAD_DIM:(h + 1) * HEAD_DIM].astype(F32) for h in range(ATT_HEADS)], l_ref[0]
        n = rows // dil
        for rho in range(dil):
            v = o_ref[rho].astype(F32)
            for h in range(ATT_HEADS):
                so_ref[h, pl.ds(rho, n, stride=dil), :] = v[:, h * HEAD_DIM:(h + 1) * HEAD_DIM]
            sl_ref[pl.ds(rho, n, stride=dil), :] = l_ref[rho]
        return [so_ref[h] for h in range(ATT_HEADS)], sl_ref[...]

    outs, lses = zip(token_order(o0_ref, l0_ref, None, None),
                     token_order(o1_ref, l1_ref, so1_ref, sl1_ref),
                     token_order(o2_ref, l2_ref, so2_ref, sl2_ref))
    top = jnp.maximum(jnp.maximum(lses[0], lses[1]), lses[2])
    es = [jnp.exp(l - top) for l in lses]
    den = es[0] + es[1] + es[2]
    wts = [e / den for e in es]
    heads = []
    for h in range(ATT_HEADS):
        acc = None
        for g in range(len(ATT_CONFIGS)):
            term = wts[g][:, h:h + 1] * outs[g][h]
            acc = term if acc is None else acc + term
        heads.append(acc)
    y_d = jnp.concatenate(heads, axis=1)

    merged = None
    for n, y in enumerate((y_a, y_b, y_c, y_d)):
        term = jax.nn.sigmoid(_dot(xm, wg_ref[n])) * _dot(y.astype(BF16), wbr_ref[n])
        merged = term if merged is None else merged + term
    u = ALPHA * x + _dot(merged.astype(BF16), wo_ref[...])
    x1 = _layer_norm(u) * g1_ref[...] + b1_ref[...]
    x1_ref[...] = x1

    x1h = x1.astype(BF16)
    x1l = (x1 - x1h.astype(F32)).astype(BF16)
    logits = _dot(x1h, rwh_ref[...]) + _dot(x1l, rwh_ref[...]) + _dot(x1h, rwl_ref[...]) + rb_ref[...]
    lane = lax.broadcasted_iota(jnp.int32, (rows, LANES), 1)
    lane_f = lane.astype(F32)
    vals, idxs = [], []
    work = logits
    for _ in range(TOP_K):
        m = jnp.max(work, axis=-1, keepdims=True)
        idx = jnp.min(jnp.where(work == m, lane_f, float(LANES)), axis=-1, keepdims=True)
        vals.append(m)
        idxs.append(idx)
        work = jnp.where(lane_f == idx, -jnp.inf, work)
    exps = [jnp.exp(v - vals[0]) for v in vals]
    tot = exps[0] + exps[1] + exps[2] + exps[3]
    ti = jnp.zeros((rows, LANES), F32)
    tp = jnp.zeros((rows, LANES), F32)
    for k in range(TOP_K):
        ti = jnp.where(lane == k, idxs[k], ti)
        tp = jnp.where(lane == k, exps[k] / tot, tp)
    hot = jnp.zeros((rows, LANES), F32)
    for k in range(TOP_K):
        hot = hot + jnp.where(lane_f == idxs[k], 1.0, 0.0)
    earlier = _dot(tri_ref[...], hot.astype(BF16))
    for k in range(TOP_K):
        rank = jnp.sum(jnp.where(lane_f == idxs[k], earlier, 0.0), axis=-1, keepdims=True)
        ti = jnp.where(lane == TOP_K + k, rank, ti)
    ti_ref[...] = ti.astype(jnp.int32)
    tp_ref[...] = tp
    cnt_ref[...] = jnp.broadcast_to(jnp.sum(hot, axis=0, keepdims=True), (SUBLANES, LANES)).astype(jnp.int32)


def _mixer(x2, seq, attn, lw):
    t, d = x2.shape
    assert d == SUBLANES * LANES
    rows = MIX_ROWS
    tps = seq // rows
    assert seq % rows == 0 and rows % CHUNK == 0
    halo_per_tile = rows // HALO
    last_halo = t // HALO - 1
    tile = lambda w: pl.BlockSpec((rows, w), lambda i: (i, 0))

    def sub_tile(arr):
        _, dil, _, w = arr.shape
        assert (rows // dil) % 16 == 0
        return pl.BlockSpec((None, dil, rows // dil, w), lambda i: (i // tps, 0, i % tps, 0))

    in_specs = [tile(d),
                pl.BlockSpec((HALO, d), lambda i: (jnp.maximum(i * halo_per_tile - 1, 0), 0)),
                pl.BlockSpec((HALO, d), lambda i: (jnp.minimum((i + 1) * halo_per_tile, last_halo), 0))]
    in_specs += [sub_tile(a[0]) for a in attn] + [sub_tile(a[1]) for a in attn]
    weights = (lw["w_abc"], lw["spatial_w"], lw["spatial_b"], lw["conv_w"], lw["pool_w"], lw["pool_scale"],
               lw["w_branch"], lw["w_gate"], lw["w_out"], lw["ln1_g"], lw["ln1_b"],
               lw["router_hi"], lw["router_lo"], lw["router_b"],
               jnp.tril(jnp.ones((rows, rows), BF16), k=-1))
    in_specs += [_resident(w.shape) for w in weights]
    interleave = [pltpu.VMEM((ATT_HEADS, rows, HEAD_DIM), F32), pltpu.VMEM((rows, LANES), F32)]
    return pl.pallas_call(
        functools.partial(_mix_kernel, rows=rows, tiles_per_seq=tps, seq=seq),
        grid=(t // rows,),
        in_specs=in_specs,
        out_specs=[tile(d), tile(LANES), tile(LANES),
                   pl.BlockSpec((None, SUBLANES, LANES), lambda i: (i, 0, 0))],
        out_shape=[jax.ShapeDtypeStruct((t, d), F32),
                   jax.ShapeDtypeStruct((t, LANES), jnp.int32),
                   jax.ShapeDtypeStruct((t, LANES), F32),
                   jax.ShapeDtypeStruct((t // rows, SUBLANES, LANES), jnp.int32)],
        scratch_shapes=[pltpu.VMEM((rows + 2 * HALO, d), BF16),
                        pltpu.VMEM((rows + 2 * HALO, B_WIDTH), F32),
                        pltpu.VMEM((rows + 2 * HALO, C_WIDTH), F32)] + interleave + interleave,
        compiler_params=_params(1),
        name="mixer",
    )(x2, x2, x2, attn[0][0], attn[1][0], attn[2][0], attn[0][1], attn[1][1], attn[2][1], *weights)


def _row_gather(table, idx):
    b = idx.shape[0]
    w = table.shape[1]
    n_workers = SC_CORES * SC_SUBCORES
    per_worker = b // n_workers
    n_chunks = per_worker // SC_CHUNK
    assert b % (8 * n_workers) == 0 and per_worker % (2 * SC_CHUNK) == 0
    mesh = plsc.VectorSubcoreMesh(core_axis_name="c", subcore_axis_name="s",
                                  num_cores=SC_CORES, num_subcores=SC_SUBCORES)

    def body(table_hbm, idx_hbm, out_hbm, idx_v, rows0, rows1, sem0, sem1):
        base = (lax.axis_index("s") * SC_CORES + lax.axis_index("c")) * per_worker
        pltpu.sync_copy(idx_hbm.at[pl.ds(base, per_worker)], idx_v)

        def gather(j, rows, sem):
            return pltpu.make_async_copy(table_hbm.at[idx_v.at[pl.ds(j * SC_CHUNK, SC_CHUNK)]], rows, sem)

        gather(0, rows0, sem0).start()

        @pl.loop(0, n_chunks, step=2)
        def _(j):
            gather(j + 1, rows1, sem1).start()
            gather(j, rows0, sem0).wait()
            pltpu.sync_copy(rows0, out_hbm.at[pl.ds(base + j * SC_CHUNK, SC_CHUNK)])

            @pl.when(j + 2 < n_chunks)
            def _():
                gather(j + 2, rows0, sem0).start()
            gather(j + 1, rows1, sem1).wait()
            pltpu.sync_copy(rows1, out_hbm.at[pl.ds(base + (j + 1) * SC_CHUNK, SC_CHUNK)])

    return pl.kernel(
        body,
        out_type=jax.ShapeDtypeStruct((b, w), table.dtype),
        mesh=mesh,
        scratch_types=[pltpu.VMEM((per_worker,), jnp.int32),
                       pltpu.VMEM((SC_CHUNK, w), table.dtype), pltpu.VMEM((SC_CHUNK, w), table.dtype),
                       pltpu.SemaphoreType.DMA, pltpu.SemaphoreType.DMA],
        name="row_gather",
    )(table, idx)


def _expert_kernel(be_ref, nu_ref, xs_ref, wgu_ref, bgu_ref, wd_ref, bd_ref, y_ref):
    b = pl.program_id(0)

    @pl.when(b < nu_ref[0])
    def _():
        xg = xs_ref[...].astype(BF16)
        h = _dot(xg, wgu_ref[...]) + bgu_ref[...]
        g = jnp.minimum(h[:, :D_FF], SWIGLU_LIMIT)
        u = jnp.clip(h[:, D_FF:], -SWIGLU_LIMIT, SWIGLU_LIMIT)
        act = g * jax.nn.sigmoid(SWIGLU_ALPHA * g) * (u + 1.0)
        y_ref[...] = _dot(act.astype(BF16), wd_ref[...]) + bd_ref[...]

    @pl.when(b >= nu_ref[0])
    def _():
        y_ref[...] = jnp.zeros(y_ref.shape, F32)


def _experts(xs, block_e, n_used, lw):
    m = EXPERT_ROWS
    d = D_MODEL
    nb = block_e.shape[0]
    rows_spec = pl.BlockSpec((m, d), lambda b, be, nu: (b, 0))
    grid_spec = pltpu.PrefetchScalarGridSpec(
        num_scalar_prefetch=2,
        grid=(nb,),
        in_specs=[rows_spec,
                  pl.BlockSpec((None, d, 2 * D_FF), lambda b, be, nu: (be[b], 0, 0)),
                  pl.BlockSpec((None, 1, 2 * D_FF), lambda b, be, nu: (be[b], 0, 0)),
                  pl.BlockSpec((None, D_FF, d), lambda b, be, nu: (be[b], 0, 0)),
                  pl.BlockSpec((None, 1, d), lambda b, be, nu: (be[b], 0, 0))],
        out_specs=rows_spec,
    )
    return pl.pallas_call(
        _expert_kernel,
        grid_spec=grid_spec,
        out_shape=jax.ShapeDtypeStruct(xs.shape, F32),
        compiler_params=_params(1),
        name="expert_blocks",
    )(block_e, n_used, xs, lw["w_gate_up"], lw["b_gate_up"], lw["w_down"], lw["b_down"])


def _combine_kernel(y0_ref, y1_ref, y2_ref, y3_ref, x1_ref, p_ref, tp_ref, wpg_ref, wpp_ref, g2_ref, b2_ref, o_ref):
    x1 = x1_ref[...]
    tp = tp_ref[...]
    moe = None
    for k, y_ref in enumerate((y0_ref, y1_ref, y2_ref, y3_ref)):
        term = tp[:, k:k + 1] * y_ref[...]
        moe = term if moe is None else moe + term
    ple = jax.nn.sigmoid(_dot(x1.astype(BF16), wpg_ref[...])) * _dot(p_ref[...].astype(BF16), wpp_ref[...])
    o_ref[...] = _layer_norm(ALPHA * x1 + moe + ple) * g2_ref[...] + b2_ref[...]


def _combine(x1, p2, top_p, y, lw):
    t, d = x1.shape
    rows = COMBINE_ROWS
    n_tiles = t // rows
    tile = lambda w: pl.BlockSpec((rows, w), lambda i: (i, 0))
    choice = lambda k: pl.BlockSpec((rows, d), lambda i: (k * n_tiles + i, 0))
    weights = (lw["w_ple_gate"], lw["w_ple_proj"], lw["ln2_g"], lw["ln2_b"])
    return pl.pallas_call(
        _combine_kernel,
        grid=(n_tiles,),
        in_specs=[choice(k) for k in range(TOP_K)] + [tile(d), tile(PLE_DIM), tile(LANES)]
        + [_resident(w.shape) for w in weights],
        out_specs=tile(d),
        out_shape=jax.ShapeDtypeStruct((t, d), F32),
        compiler_params=_params(1),
        name="combine",
    )(y, y, y, y, x1, p2, top_p, *weights)


def _routing_tables(top_i, tile_counts):
    t = top_i.shape[0]
    a = t * TOP_K
    m = EXPERT_ROWS
    nb = -(-(a + N_EXPERTS * (m - 1)) // m)
    per_tile = tile_counts[:, 0, :N_EXPERTS]
    tile_base = jnp.cumsum(per_tile, axis=0) - per_tile
    counts = jnp.sum(per_tile, axis=0)
    padded = (counts + m - 1) // m * m
    ends_padded = jnp.cumsum(padded)
    start_padded = ends_padded - padded
    block_start = jnp.arange(nb, dtype=jnp.int32) * m
    block_e = jnp.minimum(jnp.sum(block_start[:, None] >= ends_padded[None, :], axis=1, dtype=jnp.int32),
                          N_EXPERTS - 1)
    n_used = (ends_padded[-1] // m).astype(jnp.int32).reshape(1)

    base = jnp.repeat(start_padded[None, :] + tile_base, MIX_ROWS, axis=0)
    lanes = jnp.arange(N_EXPERTS, dtype=jnp.int32)[None, :]
    dest = [jnp.sum(jnp.where(top_i[:, k:k + 1] == lanes, base, 0), axis=1) + top_i[:, TOP_K + k]
            for k in range(TOP_K)]
    pos = jnp.concatenate(dest)
    choice = jnp.concatenate([jnp.arange(t, dtype=jnp.int32) + k * t for k in range(TOP_K)])
    _, by_slot = lax.sort((pos, choice), num_keys=1)
    start_sorted = jnp.cumsum(counts) - counts
    within = jnp.arange(m, dtype=jnp.int32)[None, :]
    rank = block_start[:, None] + within - start_padded[block_e][:, None]
    valid = (rank >= 0) & (rank < counts[block_e][:, None])
    src = by_slot[jnp.clip(start_sorted[block_e][:, None] + rank, 0, a - 1)]
    slot_tok = jnp.where(valid, src % t, 0).reshape(nb * m)
    return slot_tok, pos, block_e, n_used


def _layer(x, p, lw):
    bn, s, d = x.shape
    x2 = x.reshape(bn * s, d)
    attn = []
    for g, (_, dil) in enumerate(ATT_CONFIGS):
        qkv = _qkv_proj(x, lw["w_qkv"][g], dil)
        attn.append(_attention(qkv, g))
    x1, top_i, top_p, tile_counts = _mixer(x2, s, attn, lw)
    slot_tok, pos, block_e, n_used = _routing_tables(top_i, tile_counts)
    xs = _row_gather(x1, slot_tok)
    y = _experts(xs, block_e, n_used, lw)
    y = _row_gather(y, pos)
    out = _combine(x1, p.reshape(bn * s, PLE_DIM), top_p, y, lw)
    return out.reshape(bn, s, d)


def _prepare_layer(i, w_in, spatial_w, spatial_b, conv_w, pool_w, pool_scale, w_branch, w_gate, w_out, ln1_g, ln1_b,
                   router_w, router_b, w_gate_up, b_gate_up, w_down, b_down, w_ple_gate, w_ple_proj, ln2_g, ln2_b):
    rw = jnp.pad(router_w[i], ((0, 0), (0, LANES - N_EXPERTS)))
    rw_hi = rw.astype(BF16)
    row = lambda v: v[i].reshape(1, -1)
    return {
        "w_abc": w_in[i][:, :N_ABC].astype(BF16),
        "w_qkv": [w_in[i][:, N_ABC + g * N_QKV:N_ABC + (g + 1) * N_QKV].astype(BF16) for g in range(len(ATT_CONFIGS))],
        "spatial_w": spatial_w[i].astype(BF16),
        "spatial_b": jnp.broadcast_to(spatial_b[i][:, :, None], (A_GROUPS, CHUNK, A_WIDTH // A_GROUPS)),
        "conv_w": conv_w[i],
        "pool_w": pool_w[i].astype(BF16),
        "pool_scale": row(pool_scale),
        "w_branch": w_branch[i].astype(BF16),
        "w_gate": w_gate[i].astype(BF16),
        "w_out": w_out[i].astype(BF16),
        "ln1_g": row(ln1_g), "ln1_b": row(ln1_b),
        "router_hi": rw_hi,
        "router_lo": (rw - rw_hi.astype(F32)).astype(BF16),
        "router_b": jnp.pad(router_b[i], (0, LANES - N_EXPERTS), constant_values=NEG_INF).reshape(1, LANES),
        "w_gate_up": w_gate_up[i].astype(BF16),
        "b_gate_up": b_gate_up[i].reshape(N_EXPERTS, 1, 2 * D_FF),
        "w_down": w_down[i].astype(BF16),
        "b_down": b_down[i].reshape(N_EXPERTS, 1, D_MODEL),
        "w_ple_gate": w_ple_gate[i].astype(BF16),
        "w_ple_proj": w_ple_proj[i].astype(BF16),
        "ln2_g": row(ln2_g), "ln2_b": row(ln2_b),
    }


def _trunk(x, p, layers):
    for i, lw in enumerate(layers):
        x = _layer(x, p[i], lw)
    return x


def kernel(x_prompt, x_sample, p_prompt, p_sample, w_in, spatial_w, spatial_b, conv_w, pool_w, pool_scale, w_branch,
           w_gate, w_out, ln1_g, ln1_b, router_w, router_b, w_gate_up, b_gate_up, w_down, b_down, w_ple_gate,
           w_ple_proj, ln2_g, ln2_b):
    weights = (w_in, spatial_w, spatial_b, conv_w, pool_w, pool_scale, w_branch, w_gate, w_out, ln1_g, ln1_b,
               router_w, router_b, w_gate_up, b_gate_up, w_down, b_down, w_ple_gate, w_ple_proj, ln2_g, ln2_b)
    layers = [_prepare_layer(i, *weights) for i in range(w_in.shape[0])]
    return _trunk(x_prompt, p_prompt, layers), _trunk(x_sample, p_sample, layers)
```

```python
import functools

import numpy as np
import jax
import jax.numpy as jnp
from jax import lax
from jax.experimental import pallas as pl
from jax.experimental.pallas import tpu as pltpu
from jax.experimental.pallas import tpu_sc as plsc

D_MODEL = 1024
PLE_DIM = 256
CHUNK = 128
A_WIDTH = 512
A_GROUPS = 4
B_WIDTH = 512
C_WIDTH = 512
POOL_WINDOWS = (2, 4, 8, 16)
ATT_CONFIGS = ((128, 1), (512, 4), (2048, 16))
ATT_HEADS = 4
HEAD_DIM = 128
D_WIDTH = ATT_HEADS * HEAD_DIM
N_ABC = 2 * A_WIDTH + 3 * B_WIDTH + C_WIDTH
N_QKV = 3 * D_WIDTH
N_EXPERTS = 32
TOP_K = 4
D_FF = 1024
SWIGLU_LIMIT = 7.0
SWIGLU_ALPHA = 1.702
DEPTH = 2
ALPHA = (2 * DEPTH) ** 0.25
LN_EPS = 1e-5
NEG_INF = -1e30

LANES = 128
SUBLANES = 8
HALO = 8
RADIUS = 64
Q_BLOCK = 128
PROJ_ROWS = 1024
ATT_ROWS = 512
MIX_ROWS = 512
EXPERT_ROWS = 256
COMBINE_ROWS = 512
SCATTER_DEPTH = 3
SC_CORES = 2
SC_SUBCORES = 16
SC_CHUNK = 32
VMEM_LIMIT = 56 * 1024 * 1024

F32 = jnp.float32
BF16 = jnp.bfloat16


def _params(n_axes, vmem=VMEM_LIMIT):
    return pltpu.CompilerParams(dimension_semantics=("arbitrary",) * n_axes, vmem_limit_bytes=vmem)


def _resident(shape):
    zeros = (0,) * len(shape)
    return pl.BlockSpec(shape, lambda *_: zeros, pipeline_mode=pl.Buffered(1))


def _dot(a, b):
    return jnp.dot(a, b, preferred_element_type=F32)


def _layer_norm(x):
    mu = jnp.mean(x, axis=-1, keepdims=True)
    xc = x - mu
    var = jnp.mean(xc * xc, axis=-1, keepdims=True)
    return xc * lax.rsqrt(var + LN_EPS)


def _qkv_kernel(x_ref, w_ref, o_ref, *scratch, dil):
    rows, d = x_ref.shape
    if dil == 1:
        xe = x_ref[...].astype(BF16)
    else:
        lt_ref, xe_ref = scratch
        n = rows // dil
        for t in range(d // LANES):
            lt_ref[t] = x_ref[:, t * LANES:(t + 1) * LANES]
        for rho in range(dil):
            piece = jnp.concatenate([lt_ref[t, pl.ds(rho, n, stride=dil), :] for t in range(d // LANES)], axis=1)
            xe_ref[rho * n:(rho + 1) * n, :] = piece.astype(BF16)
        xe = xe_ref[...]
    y = _dot(xe, w_ref[...])
    o_ref[...] = y.astype(BF16).reshape(dil, rows // dil, N_QKV)


def _qkv_proj(x, w, dil):
    bn, s, d = x.shape
    rows = PROJ_ROWS
    assert s % rows == 0 and (rows // dil) % 16 == 0
    scratch = [] if dil == 1 else [pltpu.VMEM((d // LANES, rows, LANES), F32), pltpu.VMEM((rows, d), BF16)]
    return pl.pallas_call(
        functools.partial(_qkv_kernel, dil=dil),
        grid=(bn, s // rows),
        in_specs=[pl.BlockSpec((None, rows, d), lambda b, i: (b, i, 0)),
                  _resident((d, N_QKV))],
        out_specs=pl.BlockSpec((None, dil, rows // dil, N_QKV), lambda b, i: (b, 0, i, 0)),
        out_shape=jax.ShapeDtypeStruct((bn, dil, s // dil, N_QKV), BF16),
        scratch_shapes=scratch,
        compiler_params=_params(2),
        name=f"qkv_proj_d{dil}",
    )(x, w)


def _attn_kernel(main_ref, kp_ref, vp_ref, kn_ref, vn_ref, o_ref, lse_ref, kbuf, vbuf, *, sub, rows, dil, slopes):
    i = pl.program_id(2)
    kbuf[0:RADIUS, :] = kp_ref[...]
    kbuf[RADIUS:RADIUS + rows, :] = main_ref[:, D_WIDTH:2 * D_WIDTH]
    kbuf[RADIUS + rows:, :] = kn_ref[...]
    vbuf[0:RADIUS, :] = vp_ref[...]
    vbuf[RADIUS:RADIUS + rows, :] = main_ref[:, 2 * D_WIDTH:3 * D_WIDTH]
    vbuf[RADIUS + rows:, :] = vn_ref[...]

    window = Q_BLOCK + 2 * RADIUS
    row = lax.broadcasted_iota(jnp.int32, (Q_BLOCK, window), 0)
    col = lax.broadcasted_iota(jnp.int32, (Q_BLOCK, window), 1)
    dist = jnp.abs(col - RADIUS - row)
    in_band = dist <= RADIUS
    dist_f = (dist * dil).astype(F32)
    lane = lax.broadcasted_iota(jnp.int32, (Q_BLOCK, LANES), 1)
    scale = HEAD_DIM ** -0.5
    for j in range(rows // Q_BLOCK):
        kpos = i * rows + j * Q_BLOCK - RADIUS + col
        valid = in_band & (kpos >= 0) & (kpos < sub)
        lse_tile = jnp.zeros((Q_BLOCK, LANES), F32)
        for h in range(ATT_HEADS):
            hs = slice(h * HEAD_DIM, (h + 1) * HEAD_DIM)
            q = main_ref[j * Q_BLOCK:(j + 1) * Q_BLOCK, hs]
            k = kbuf[j * Q_BLOCK:j * Q_BLOCK + window, hs]
            v = vbuf[j * Q_BLOCK:j * Q_BLOCK + window, hs]
            s = lax.dot_general(q, k, (((1,), (1,)), ((), ())), preferred_element_type=F32)
            s = s * scale + (-slopes[h]) * dist_f
            s = jnp.where(valid, s, NEG_INF)
            m = jnp.max(s, axis=-1, keepdims=True)
            p = jnp.exp(s - m)
            l = jnp.sum(p, axis=-1, keepdims=True)
            o = _dot(p.astype(BF16), v) / l
            o_ref[j * Q_BLOCK:(j + 1) * Q_BLOCK, hs] = o.astype(BF16)
            lse_tile = jnp.where(lane == h, m + jnp.log(l), lse_tile)
        lse_ref[j * Q_BLOCK:(j + 1) * Q_BLOCK, :] = lse_tile


def _attention(qkv, group):
    bn, dil, sub, _ = qkv.shape
    rows = min(sub, ATT_ROWS)
    assert sub % rows == 0 and rows % Q_BLOCK == 0 and sub % RADIUS == 0
    n_groups = len(ATT_CONFIGS) * ATT_HEADS
    slopes = tuple(float(np.float32(2.0 ** (-8.0 * (group * ATT_HEADS + h + 1) / n_groups))) for h in range(ATT_HEADS))
    halo_per_tile = rows // RADIUS
    last_halo = sub // RADIUS - 1

    def prev_map(col):
        return lambda b, r, i: (b, r, jnp.maximum(i * halo_per_tile - 1, 0), col)

    def next_map(col):
        return lambda b, r, i: (b, r, jnp.minimum((i + 1) * halo_per_tile, last_halo), col)

    return pl.pallas_call(
        functools.partial(_attn_kernel, sub=sub, rows=rows, dil=dil, slopes=slopes),
        grid=(bn, dil, sub // rows),
        in_specs=[pl.BlockSpec((None, None, rows, N_QKV), lambda b, r, i: (b, r, i, 0)),
                  pl.BlockSpec((None, None, RADIUS, D_WIDTH), prev_map(1)),
                  pl.BlockSpec((None, None, RADIUS, D_WIDTH), prev_map(2)),
                  pl.BlockSpec((None, None, RADIUS, D_WIDTH), next_map(1)),
                  pl.BlockSpec((None, None, RADIUS, D_WIDTH), next_map(2))],
        out_specs=[pl.BlockSpec((None, None, rows, D_WIDTH), lambda b, r, i: (b, r, i, 0)),
                   pl.BlockSpec((None, None, rows, LANES), lambda b, r, i: (b, r, i, 0))],
        out_shape=[jax.ShapeDtypeStruct((bn, dil, sub, D_WIDTH), BF16),
                   jax.ShapeDtypeStruct((bn, dil, sub, LANES), F32)],
        scratch_shapes=[pltpu.VMEM((rows + 2 * RADIUS, D_WIDTH), BF16),
                        pltpu.VMEM((rows + 2 * RADIUS, D_WIDTH), BF16)],
        compiler_params=_params(3),
        name=f"band_attention_d{dil}",
    )(qkv, qkv, qkv, qkv, qkv)


def _mix_kernel(x_ref, xp_ref, xn_ref, o0_ref, o1_ref, o2_ref, l0_ref, l1_ref, l2_ref,
                win_ref, ws_ref, bs_ref, cw_ref, pw_ref, ps_ref, wbr_ref, wg_ref, wo_ref,
                g1_ref, b1_ref, rwh_ref, rwl_ref, rb_ref,
                x1_ref, ti_ref, tp_ref,
                xe_ref, zb_ref, cb_ref, so1_ref, sl1_ref, so2_ref, sl2_ref, *, rows, tiles_per_seq, seq):
    tis = pl.program_id(0) % tiles_per_seq
    has_prev = tis > 0
    has_next = tis < tiles_per_seq - 1

    x = x_ref[...]
    xe_ref[0:rows, :] = x.astype(BF16)
    xe_ref[rows:rows + 2 * HALO, :] = jnp.concatenate([xp_ref[...], xn_ref[...]], axis=0).astype(BF16)
    xm = xe_ref[0:rows, :]

    h_a = _dot(xm, win_ref[:, 0:2 * A_WIDTH])
    a_u = h_a[:, 0:A_WIDTH]
    vn = _layer_norm(h_a[:, A_WIDTH:2 * A_WIDTH]).astype(BF16)
    gw = A_WIDTH // A_GROUPS
    chunks = []
    for c in range(rows // CHUNK):
        cols = [_dot(ws_ref[g], vn[c * CHUNK:(c + 1) * CHUNK, g * gw:(g + 1) * gw]) + bs_ref[g]
                for g in range(A_GROUPS)]
        chunks.append(jnp.concatenate(cols, axis=1))
    y_a = a_u * jnp.concatenate(chunks, axis=0)

    h_bc = _dot(xe_ref[...], win_ref[:, 2 * A_WIDTH:N_ABC])
    b_b = h_bc[0:rows, B_WIDTH:2 * B_WIDTH]
    z = h_bc[:, 2 * B_WIDTH:3 * B_WIDTH] * h_bc[:, 0:B_WIDTH]
    cz = h_bc[:, 3 * B_WIDTH:3 * B_WIDTH + C_WIDTH]
    for buf, val in ((zb_ref, z), (cb_ref, cz)):
        buf[0:HALO, :] = jnp.where(has_prev, val[rows:rows + HALO], 0.0)
        buf[HALO:HALO + rows, :] = val[0:rows]
        buf[HALO + rows:, :] = jnp.where(has_next, val[rows + HALO:], 0.0)

    conv = (cw_ref[0:1, :] * zb_ref[HALO - 1:HALO - 1 + rows, :]
            + cw_ref[1:2, :] * zb_ref[HALO:HALO + rows, :]
            + cw_ref[2:3, :] * zb_ref[HALO + 1:HALO + 1 + rows, :])
    y_b = b_b * conv

    pos = tis * rows + lax.broadcasted_iota(jnp.int32, (rows, 1), 0)
    cgw = C_WIDTH // len(POOL_WINDOWS)
    pooled_out = []
    for g, w in enumerate(POOL_WINDOWS):
        cs = slice(g * cgw, (g + 1) * cgw)
        acc = None
        for dd in range(-(w // 2), w // 2):
            v = cb_ref[HALO + dd:HALO + dd + rows, cs]
            acc = v if acc is None else acc + v
        count = (jnp.minimum(pos + w // 2, seq) - jnp.maximum(pos - w // 2, 0)).astype(F32)
        pooled = acc / count - cb_ref[HALO:HALO + rows, cs]
        pooled_out.append(_dot(pooled.astype(BF16), pw_ref[g]))
    y_c = jnp.concatenate(pooled_out, axis=1) * ps_ref[...]

    def token_order(o_ref, l_ref, so_ref, sl_ref):
        dil = o_ref.shape[0]
        if dil == 1:
            return [o_ref[0, :, h * HEAD_DIM:(h + 1) * HEAD_DIM].astype(F32) for h in range(ATT_HEADS)], l_ref[0]
        n = rows // dil
        for rho in range(dil):
            v = o_ref[rho].astype(F32)
            for h in range(ATT_HEADS):
                so_ref[h, pl.ds(rho, n, stride=dil), :] = v[:, h * HEAD_DIM:(h + 1) * HEAD_DIM]
            sl_ref[pl.ds(rho, n, stride=dil), :] = l_ref[rho]
        return [so_ref[h] for h in range(ATT_HEADS)], sl_ref[...]

    outs, lses = zip(token_order(o0_ref, l0_ref, None, None),
                     token_order(o1_ref, l1_ref, so1_ref, sl1_ref),
                     token_order(o2_ref, l2_ref, so2_ref, sl2_ref))
    top = jnp.maximum(jnp.maximum(lses[0], lses[1]), lses[2])
    es = [jnp.exp(l - top) for l in lses]
    den = es[0] + es[1] + es[2]
    wts = [e / den for e in es]
    heads = []
    for h in range(ATT_HEADS):
        acc = None
        for g in range(len(ATT_CONFIGS)):
            term = wts[g][:, h:h + 1] * outs[g][h]
            acc = term if acc is None else acc + term
        heads.append(acc)
    y_d = jnp.concatenate(heads, axis=1)

    merged = None
    for n, y in enumerate((y_a, y_b, y_c, y_d)):
        term = jax.nn.sigmoid(_dot(xm, wg_ref[n])) * _dot(y.astype(BF16), wbr_ref[n])
        merged = term if merged is None else merged + term
    u = ALPHA * x + _dot(merged.astype(BF16), wo_ref[...])
    x1 = _layer_norm(u) * g1_ref[...] + b1_ref[...]
    x1_ref[...] = x1

    x1h = x1.astype(BF16)
    x1l = (x1 - x1h.astype(F32)).astype(BF16)
    logits = _dot(x1h, rwh_ref[...]) + _dot(x1l, rwh_ref[...]) + _dot(x1h, rwl_ref[...]) + rb_ref[...]
    lane = lax.broadcasted_iota(jnp.int32, (rows, LANES), 1)
    lane_f = lane.astype(F32)
    vals, idxs = [], []
    work = logits
    for _ in range(TOP_K):
        m = jnp.max(work, axis=-1, keepdims=True)
        idx = jnp.min(jnp.where(work == m, lane_f, float(LANES)), axis=-1, keepdims=True)
        vals.append(m)
        idxs.append(idx)
        work = jnp.where(lane_f == idx, -jnp.inf, work)
    exps = [jnp.exp(v - vals[0]) for v in vals]
    tot = exps[0] + exps[1] + exps[2] + exps[3]
    ti = jnp.zeros((rows, LANES), F32)
    tp = jnp.zeros((rows, LANES), F32)
    for k in range(TOP_K):
        ti = jnp.where(lane == k, idxs[k], ti)
        tp = jnp.where(lane == k, exps[k] / tot, tp)
    ti_ref[...] = ti.astype(jnp.int32)
    tp_ref[...] = tp


def _mixer(x2, seq, attn, lw):
    t, d = x2.shape
    rows = MIX_ROWS
    tps = seq // rows
    assert seq % rows == 0 and rows % CHUNK == 0
    halo_per_tile = rows // HALO
    last_halo = t // HALO - 1
    tile = lambda w: pl.BlockSpec((rows, w), lambda i: (i, 0))

    def sub_tile(arr):
        _, dil, _, w = arr.shape
        assert (rows // dil) % 16 == 0
        return pl.BlockSpec((None, dil, rows // dil, w), lambda i: (i // tps, 0, i % tps, 0))

    in_specs = [tile(d),
                pl.BlockSpec((HALO, d), lambda i: (jnp.maximum(i * halo_per_tile - 1, 0), 0)),
                pl.BlockSpec((HALO, d), lambda i: (jnp.minimum((i + 1) * halo_per_tile, last_halo), 0))]
    in_specs += [sub_tile(a[0]) for a in attn] + [sub_tile(a[1]) for a in attn]
    weights = (lw["w_abc"], lw["spatial_w"], lw["spatial_b"], lw["conv_w"], lw["pool_w"], lw["pool_scale"],
               lw["w_branch"], lw["w_gate"], lw["w_out"], lw["ln1_g"], lw["ln1_b"],
               lw["router_hi"], lw["router_lo"], lw["router_b"])
    in_specs += [_resident(w.shape) for w in weights]
    interleave = [pltpu.VMEM((ATT_HEADS, rows, HEAD_DIM), F32), pltpu.VMEM((rows, LANES), F32)]
    return pl.pallas_call(
        functools.partial(_mix_kernel, rows=rows, tiles_per_seq=tps, seq=seq),
        grid=(t // rows,),
        in_specs=in_specs,
        out_specs=[tile(d), tile(LANES), tile(LANES)],
        out_shape=[jax.ShapeDtypeStruct((t, d), F32),
                   jax.ShapeDtypeStruct((t, LANES), jnp.int32),
                   jax.ShapeDtypeStruct((t, LANES), F32)],
        scratch_shapes=[pltpu.VMEM((rows + 2 * HALO, d), BF16),
                        pltpu.VMEM((rows + 2 * HALO, B_WIDTH), F32),
                        pltpu.VMEM((rows + 2 * HALO, C_WIDTH), F32)] + interleave + interleave,
        compiler_params=_params(1),
        name="mixer",
    )(x2, x2, x2, attn[0][0], attn[1][0], attn[2][0], attn[0][1], attn[1][1], attn[2][1], *weights)


def _row_gather(table, idx):
    b = idx.shape[0]
    w = table.shape[1]
    n_workers = SC_CORES * SC_SUBCORES
    per_worker = b // n_workers
    n_chunks = per_worker // SC_CHUNK
    assert b % (8 * n_workers) == 0 and per_worker % (2 * SC_CHUNK) == 0
    mesh = plsc.VectorSubcoreMesh(core_axis_name="c", subcore_axis_name="s",
                                  num_cores=SC_CORES, num_subcores=SC_SUBCORES)

    def body(table_hbm, idx_hbm, out_hbm, idx_v, rows0, rows1, sem0, sem1):
        base = (lax.axis_index("s") * SC_CORES + lax.axis_index("c")) * per_worker
        pltpu.sync_copy(idx_hbm.at[pl.ds(base, per_worker)], idx_v)

        def gather(j, rows, sem):
            return pltpu.make_async_copy(table_hbm.at[idx_v.at[pl.ds(j * SC_CHUNK, SC_CHUNK)]], rows, sem)

        gather(0, rows0, sem0).start()

        @pl.loop(0, n_chunks, step=2)
        def _(j):
            gather(j + 1, rows1, sem1).start()
            gather(j, rows0, sem0).wait()
            pltpu.sync_copy(rows0, out_hbm.at[pl.ds(base + j * SC_CHUNK, SC_CHUNK)])

            @pl.when(j + 2 < n_chunks)
            def _():
                gather(j + 2, rows0, sem0).start()
            gather(j + 1, rows1, sem1).wait()
            pltpu.sync_copy(rows1, out_hbm.at[pl.ds(base + (j + 1) * SC_CHUNK, SC_CHUNK)])

    return pl.kernel(
        body,
        out_type=jax.ShapeDtypeStruct((b, w), table.dtype),
        mesh=mesh,
        scratch_types=[pltpu.VMEM((per_worker,), jnp.int32),
                       pltpu.VMEM((SC_CHUNK, w), table.dtype), pltpu.VMEM((SC_CHUNK, w), table.dtype),
                       pltpu.SemaphoreType.DMA, pltpu.SemaphoreType.DMA],
        name="row_gather",
    )(table, idx)


def _expert_kernel(be_ref, nu_ref, dst_ref, xs_ref, wgu_ref, bgu_ref, wd_ref, bd_ref, y_hbm, obuf, osem, *, n_blocks):
    m = EXPERT_ROWS
    b = pl.program_id(0)
    n_used = nu_ref[0]

    def scatter_wait(blk):
        slot = blk % 2
        pltpu.make_async_copy(obuf.at[slot], y_hbm.at[pl.ds(0, m)], osem.at[slot]).wait()

    @pl.when(b == 0)
    def _():
        obuf[0] = jnp.zeros(obuf.shape[1:], F32)
        n_rows = y_hbm.shape[0]
        for j in range(SCATTER_DEPTH):
            dump = pltpu.make_async_copy(obuf.at[0], y_hbm.at[pl.ds(n_rows - (j + 1) * m, m)], osem.at[0])
            dump.start()
            dump.wait()

    @pl.when((b >= 2) & (b - 2 < n_used))
    def _():
        scatter_wait(b - 2)

    @pl.when(b < n_used)
    def _():
        slot = b % 2
        xg = xs_ref[...].astype(BF16)
        h = _dot(xg, wgu_ref[...]) + bgu_ref[...]
        g = jnp.minimum(h[:, :D_FF], SWIGLU_LIMIT)
        u = jnp.clip(h[:, D_FF:], -SWIGLU_LIMIT, SWIGLU_LIMIT)
        act = g * jax.nn.sigmoid(SWIGLU_ALPHA * g) * (u + 1.0)
        y = _dot(act.astype(BF16), wd_ref[...]) + bd_ref[...]
        obuf[slot] = y.reshape(m, SUBLANES, LANES)
        for i in range(m):
            pltpu.make_async_copy(obuf.at[slot, pl.ds(i, 1)], y_hbm.at[pl.ds(dst_ref[0, 0, i], 1)],
                                  osem.at[slot]).start()

    @pl.when(b == n_blocks - 1)
    def _():
        @pl.when((b >= 1) & (b - 1 < n_used))
        def _():
            scatter_wait(b - 1)

        @pl.when(b < n_used)
        def _():
            scatter_wait(b)


def _experts(xs, slot_dst, block_e, n_used, n_tokens, lw):
    m = EXPERT_ROWS
    d = D_MODEL
    nb = block_e.shape[0]
    grid_spec = pltpu.PrefetchScalarGridSpec(
        num_scalar_prefetch=2,
        grid=(nb,),
        in_specs=[pl.BlockSpec((1, 1, m), lambda b, be, nu: (b, 0, 0), memory_space=pltpu.SMEM),
                  pl.BlockSpec((m, d), lambda b, be, nu: (b, 0)),
                  pl.BlockSpec((None, d, 2 * D_FF), lambda b, be, nu: (be[b], 0, 0)),
                  pl.BlockSpec((None, 1, 2 * D_FF), lambda b, be, nu: (be[b], 0, 0)),
                  pl.BlockSpec((None, D_FF, d), lambda b, be, nu: (be[b], 0, 0)),
                  pl.BlockSpec((None, 1, d), lambda b, be, nu: (be[b], 0, 0))],
        out_specs=pl.BlockSpec(memory_space=pl.ANY),
        scratch_shapes=[pltpu.VMEM((2, m, SUBLANES, LANES), F32), pltpu.SemaphoreType.DMA((2,))],
    )
    return pl.pallas_call(
        functools.partial(_expert_kernel, n_blocks=nb),
        grid_spec=grid_spec,
        out_shape=jax.ShapeDtypeStruct((TOP_K * n_tokens + SCATTER_DEPTH * m, SUBLANES, LANES), F32),
        compiler_params=_params(1),
        name="expert_blocks",
    )(block_e, n_used, slot_dst, xs, lw["w_gate_up"], lw["b_gate_up"], lw["w_down"], lw["b_down"])


def _combine_kernel(y0_ref, y1_ref, y2_ref, y3_ref, x1_ref, p_ref, tp_ref, wpg_ref, wpp_ref, g2_ref, b2_ref, o_ref):
    rows = x1_ref.shape[0]
    x1 = x1_ref[...]
    tp = tp_ref[...]
    moe = None
    for k, y_ref in enumerate((y0_ref, y1_ref, y2_ref, y3_ref)):
        term = tp[:, k:k + 1] * y_ref[...].reshape(rows, SUBLANES * LANES)
        moe = term if moe is None else moe + term
    ple = jax.nn.sigmoid(_dot(x1.astype(BF16), wpg_ref[...])) * _dot(p_ref[...].astype(BF16), wpp_ref[...])
    o_ref[...] = _layer_norm(ALPHA * x1 + moe + ple) * g2_ref[...] + b2_ref[...]


def _combine(x1, p2, top_p, y, lw):
    t, d = x1.shape
    rows = COMBINE_ROWS
    n_tiles = t // rows
    tile = lambda w: pl.BlockSpec((rows, w), lambda i: (i, 0))
    choice = lambda k: pl.BlockSpec((rows, SUBLANES, LANES), lambda i: (k * n_tiles + i, 0, 0))
    weights = (lw["w_ple_gate"], lw["w_ple_proj"], lw["ln2_g"], lw["ln2_b"])
    return pl.pallas_call(
        _combine_kernel,
        grid=(n_tiles,),
        in_specs=[choice(k) for k in range(TOP_K)] + [tile(d), tile(PLE_DIM), tile(LANES)]
        + [_resident(w.shape) for w in weights],
        out_specs=tile(d),
        out_shape=jax.ShapeDtypeStruct((t, d), F32),
        compiler_params=_params(1),
        name="combine",
    )(y, y, y, y, x1, p2, top_p, *weights)


def _routing_tables(top_i):
    t = top_i.shape[0]
    a = t * TOP_K
    m = EXPERT_ROWS
    nb = -(-(a + N_EXPERTS * (m - 1)) // m)
    e_flat = top_i.reshape(a)
    keys = jnp.sort(e_flat * a + jnp.arange(a, dtype=jnp.int32))
    a_sorted = keys % a
    experts = jnp.arange(N_EXPERTS, dtype=jnp.int32)
    start_sorted = jnp.sum(keys[None, :] < (experts * a)[:, None], axis=1, dtype=jnp.int32)
    counts = jnp.concatenate([start_sorted[1:], jnp.full((1,), a, jnp.int32)]) - start_sorted
    padded = (counts + m - 1) // m * m
    ends_padded = jnp.cumsum(padded)
    start_padded = ends_padded - padded
    block_start = jnp.arange(nb, dtype=jnp.int32) * m
    block_e = jnp.minimum(jnp.sum(block_start[:, None] >= ends_padded[None, :], axis=1, dtype=jnp.int32),
                          N_EXPERTS - 1)
    n_used = (ends_padded[-1] // m).astype(jnp.int32).reshape(1)
    within = jnp.arange(m, dtype=jnp.int32)[None, :]
    rank = block_start[:, None] + within - start_padded[block_e][:, None]
    valid = (rank >= 0) & (rank < counts[block_e][:, None])
    src = a_sorted[jnp.clip(start_sorted[block_e][:, None] + rank, 0, a - 1)]
    slot_tok = jnp.where(valid, src // TOP_K, 0)
    dump = a + (jnp.arange(nb, dtype=jnp.int32) % SCATTER_DEPTH)[:, None] * m + within
    slot_dst = jnp.where(valid, (src % TOP_K) * t + src // TOP_K, dump)
    return slot_tok.reshape(nb * m), slot_dst.reshape(nb, 1, m), block_e, n_used


def _layer(x, p, lw):
    bn, s, d = x.shape
    x2 = x.reshape(bn * s, d)
    attn = []
    for g, (_, dil) in enumerate(ATT_CONFIGS):
        qkv = _qkv_proj(x, lw["w_qkv"][g], dil)
        attn.append(_attention(qkv, g))
    x1, top_i, top_p = _mixer(x2, s, attn, lw)
    slot_tok, slot_dst, block_e, n_used = _routing_tables(top_i[:, :TOP_K])
    xs = _row_gather(x1, slot_tok)
    y = _experts(xs, slot_dst, block_e, n_used, bn * s, lw)
    out = _combine(x1, p.reshape(bn * s, PLE_DIM), top_p, y, lw)
    return out.reshape(bn, s, d)


def _prepare_layer(i, w_in, spatial_w, spatial_b, conv_w, pool_w, pool_scale, w_branch, w_gate, w_out, ln1_g, ln1_b,
                   router_w, router_b, w_gate_up, b_gate_up, w_down, b_down, w_ple_gate, w_ple_proj, ln2_g, ln2_b):
    rw = jnp.pad(router_w[i], ((0, 0), (0, LANES - N_EXPERTS)))
    rw_hi = rw.astype(BF16)
    row = lambda v: v[i].reshape(1, -1)
    return {
        "w_abc": w_in[i][:, :N_ABC].astype(BF16),
        "w_qkv": [w_in[i][:, N_ABC + g * N_QKV:N_ABC + (g + 1) * N_QKV].astype(BF16) for g in range(len(ATT_CONFIGS))],
        "spatial_w": spatial_w[i].astype(BF16),
        "spatial_b": jnp.broadcast_to(spatial_b[i][:, :, None], (A_GROUPS, CHUNK, A_WIDTH // A_GROUPS)),
        "conv_w": conv_w[i],
        "pool_w": pool_w[i].astype(BF16),
        "pool_scale": row(pool_scale),
        "w_branch": w_branch[i].astype(BF16),
        "w_gate": w_gate[i].astype(BF16),
        "w_out": w_out[i].astype(BF16),
        "ln1_g": row(ln1_g), "ln1_b": row(ln1_b),
        "router_hi": rw_hi,
        "router_lo": (rw - rw_hi.astype(F32)).astype(BF16),
        "router_b": jnp.pad(router_b[i], (0, LANES - N_EXPERTS), constant_values=NEG_INF).reshape(1, LANES),
        "w_gate_up": w_gate_up[i].astype(BF16),
        "b_gate_up": b_gate_up[i].reshape(N_EXPERTS, 1, 2 * D_FF),
        "w_down": w_down[i].astype(BF16),
        "b_down": b_down[i].reshape(N_EXPERTS, 1, D_MODEL),
        "w_ple_gate": w_ple_gate[i].astype(BF16),
        "w_ple_proj": w_ple_proj[i].astype(BF16),
        "ln2_g": row(ln2_g), "ln2_b": row(ln2_b),
    }


def _trunk(x, p, layers):
    for i, lw in enumerate(layers):
        x = _layer(x, p[i], lw)
    return x


def kernel(x_prompt, x_sample, p_prompt, p_sample, w_in, spatial_w, spatial_b, conv_w, pool_w, pool_scale, w_branch,
           w_gate, w_out, ln1_g, ln1_b, router_w, router_b, w_gate_up, b_gate_up, w_down, b_down, w_ple_gate,
           w_ple_proj, ln2_g, ln2_b):
    weights = (w_in, spatial_w, spatial_b, conv_w, pool_w, pool_scale, w_branch, w_gate, w_out, ln1_g, ln1_b,
               router_w, router_b, w_gate_up, b_gate_up, w_down, b_down, w_ple_gate, w_ple_proj, ln2_g, ln2_b)
    layers = [_prepare_layer(i, *weights) for i in range(w_in.shape[0])]
    return _trunk(x_prompt, p_prompt, layers), _trunk(x_sample, p_sample, layers)
```

```python
import functools

import numpy as np
import jax
import jax.numpy as jnp
from jax import lax
from jax.experimental import pallas as pl
from jax.experimental.pallas import tpu as pltpu

D_MODEL = 1024
PLE_DIM = 256
CHUNK = 128
A_WIDTH = 512
A_GROUPS = 4
B_WIDTH = 512
C_WIDTH = 512
POOL_WINDOWS = (2, 4, 8, 16)
ATT_CONFIGS = ((128, 1), (512, 4), (2048, 16))
ATT_HEADS = 4
HEAD_DIM = 128
D_WIDTH = ATT_HEADS * HEAD_DIM
N_ABC = 2 * A_WIDTH + 3 * B_WIDTH + C_WIDTH
N_QKV = 3 * D_WIDTH
N_EXPERTS = 32
TOP_K = 4
D_FF = 1024
SWIGLU_LIMIT = 7.0
SWIGLU_ALPHA = 1.702
DEPTH = 2
ALPHA = (2 * DEPTH) ** 0.25
LN_EPS = 1e-5
NEG_INF = -1e30

LANES = 128
SUBLANES = 8
HALO = 8
RADIUS = 64
Q_BLOCK = 128
PROJ_ROWS = 1024
ATT_ROWS = 512
MIX_ROWS = 512
EXPERT_ROWS = 256
COMBINE_ROWS = 512
SCATTER_DEPTH = 3
VMEM_LIMIT = 56 * 1024 * 1024

F32 = jnp.float32
BF16 = jnp.bfloat16


def _params(n_axes, vmem=VMEM_LIMIT):
    return pltpu.CompilerParams(dimension_semantics=("arbitrary",) * n_axes, vmem_limit_bytes=vmem)


def _resident(shape):
    zeros = (0,) * len(shape)
    return pl.BlockSpec(shape, lambda *_: zeros, pipeline_mode=pl.Buffered(1))


def _dot(a, b):
    return jnp.dot(a, b, preferred_element_type=F32)


def _layer_norm(x):
    mu = jnp.mean(x, axis=-1, keepdims=True)
    xc = x - mu
    var = jnp.mean(xc * xc, axis=-1, keepdims=True)
    return xc * lax.rsqrt(var + LN_EPS)


def _qkv_kernel(x_ref, w_ref, o_ref, *scratch, dil):
    rows, d = x_ref.shape
    if dil == 1:
        xe = x_ref[...].astype(BF16)
    else:
        lt_ref, xe_ref = scratch
        n = rows // dil
        for t in range(d // LANES):
            lt_ref[t] = x_ref[:, t * LANES:(t + 1) * LANES]
        for rho in range(dil):
            piece = jnp.concatenate([lt_ref[t, pl.ds(rho, n, stride=dil), :] for t in range(d // LANES)], axis=1)
            xe_ref[rho * n:(rho + 1) * n, :] = piece.astype(BF16)
        xe = xe_ref[...]
    y = _dot(xe, w_ref[...])
    o_ref[...] = y.astype(BF16).reshape(dil, rows // dil, N_QKV)


def _qkv_proj(x, w, dil):
    bn, s, d = x.shape
    rows = PROJ_ROWS
    assert s % rows == 0 and (rows // dil) % 16 == 0
    scratch = [] if dil == 1 else [pltpu.VMEM((d // LANES, rows, LANES), F32), pltpu.VMEM((rows, d), BF16)]
    return pl.pallas_call(
        functools.partial(_qkv_kernel, dil=dil),
        grid=(bn, s // rows),
        in_specs=[pl.BlockSpec((None, rows, d), lambda b, i: (b, i, 0)),
                  _resident((d, N_QKV))],
        out_specs=pl.BlockSpec((None, dil, rows // dil, N_QKV), lambda b, i: (b, 0, i, 0)),
        out_shape=jax.ShapeDtypeStruct((bn, dil, s // dil, N_QKV), BF16),
        scratch_shapes=scratch,
        compiler_params=_params(2),
        name=f"qkv_proj_d{dil}",
    )(x, w)


def _attn_kernel(*refs, sub, rows, dil, slopes, halo):
    if halo:
        main_ref, kp_ref, vp_ref, kn_ref, vn_ref, o_ref, lse_ref, kbuf, vbuf = refs
    else:
        main_ref, o_ref, lse_ref, kbuf, vbuf = refs
        for buf in (kbuf, vbuf):
            buf[0:RADIUS, :] = jnp.zeros((RADIUS, D_WIDTH), BF16)
            buf[RADIUS + rows:, :] = jnp.zeros((RADIUS, D_WIDTH), BF16)
    i = pl.program_id(2)

    window = Q_BLOCK + 2 * RADIUS
    row = lax.broadcasted_iota(jnp.int32, (Q_BLOCK, window), 0)
    col = lax.broadcasted_iota(jnp.int32, (Q_BLOCK, window), 1)
    dist = jnp.abs(col - RADIUS - row)
    in_band = dist <= RADIUS
    dist_f = (dist * dil).astype(F32)
    lane = lax.broadcasted_iota(jnp.int32, (Q_BLOCK, LANES), 1)
    scale = HEAD_DIM ** -0.5
    for n in range(main_ref.shape[0]):
        if halo:
            kbuf[0:RADIUS, :] = kp_ref[...]
            kbuf[RADIUS + rows:, :] = kn_ref[...]
            vbuf[0:RADIUS, :] = vp_ref[...]
            vbuf[RADIUS + rows:, :] = vn_ref[...]
        kbuf[RADIUS:RADIUS + rows, :] = main_ref[n, :, D_WIDTH:2 * D_WIDTH]
        vbuf[RADIUS:RADIUS + rows, :] = main_ref[n, :, 2 * D_WIDTH:3 * D_WIDTH]
        for j in range(rows // Q_BLOCK):
            kpos = i * rows + j * Q_BLOCK - RADIUS + col
            valid = in_band & (kpos >= 0) & (kpos < sub)
            lse_tile = jnp.zeros((Q_BLOCK, LANES), F32)
            for h in range(ATT_HEADS):
                hs = slice(h * HEAD_DIM, (h + 1) * HEAD_DIM)
                q = main_ref[n, j * Q_BLOCK:(j + 1) * Q_BLOCK, hs]
                k = kbuf[j * Q_BLOCK:j * Q_BLOCK + window, hs]
                v = vbuf[j * Q_BLOCK:j * Q_BLOCK + window, hs]
                s = lax.dot_general(q, k, (((1,), (1,)), ((), ())), preferred_element_type=F32)
                s = s * scale + (-slopes[h]) * dist_f
                s = jnp.where(valid, s, NEG_INF)
                m = jnp.max(s, axis=-1, keepdims=True)
                p = jnp.exp(s - m)
                l = jnp.sum(p, axis=-1, keepdims=True)
                o = _dot(p.astype(BF16), v) / l
                o_ref[n, j * Q_BLOCK:(j + 1) * Q_BLOCK, hs] = o.astype(BF16)
                lse_tile = jnp.where(lane == h, m + jnp.log(l), lse_tile)
            lse_ref[n, j * Q_BLOCK:(j + 1) * Q_BLOCK, :] = lse_tile


def _attention(qkv, group):
    bn, dil, sub, _ = qkv.shape
    rows = min(sub, ATT_ROWS)
    assert sub % rows == 0 and rows % Q_BLOCK == 0 and sub % RADIUS == 0
    n_groups = len(ATT_CONFIGS) * ATT_HEADS
    slopes = tuple(float(np.float32(2.0 ** (-8.0 * (group * ATT_HEADS + h + 1) / n_groups))) for h in range(ATT_HEADS))
    halo = sub > rows
    n_seq = 1 if halo else min(dil, ATT_ROWS // rows)
    assert dil % n_seq == 0
    halo_per_tile = rows // RADIUS
    last_halo = sub // RADIUS - 1

    def prev_map(col):
        return lambda b, r, i: (b, r, jnp.maximum(i * halo_per_tile - 1, 0), col)

    def next_map(col):
        return lambda b, r, i: (b, r, jnp.minimum((i + 1) * halo_per_tile, last_halo), col)

    tile = lambda w: pl.BlockSpec((None, n_seq, rows, w), lambda b, r, i: (b, r, i, 0))
    in_specs = [tile(N_QKV)]
    if halo:
        in_specs += [pl.BlockSpec((None, None, RADIUS, D_WIDTH), prev_map(1)),
                     pl.BlockSpec((None, None, RADIUS, D_WIDTH), prev_map(2)),
                     pl.BlockSpec((None, None, RADIUS, D_WIDTH), next_map(1)),
                     pl.BlockSpec((None, None, RADIUS, D_WIDTH), next_map(2))]
    return pl.pallas_call(
        functools.partial(_attn_kernel, sub=sub, rows=rows, dil=dil, slopes=slopes, halo=halo),
        grid=(bn, dil // n_seq, sub // rows),
        in_specs=in_specs,
        out_specs=[tile(D_WIDTH), tile(LANES)],
        out_shape=[jax.ShapeDtypeStruct((bn, dil, sub, D_WIDTH), BF16),
                   jax.ShapeDtypeStruct((bn, dil, sub, LANES), F32)],
        scratch_shapes=[pltpu.VMEM((rows + 2 * RADIUS, D_WIDTH), BF16),
                        pltpu.VMEM((rows + 2 * RADIUS, D_WIDTH), BF16)],
        compiler_params=_params(3),
        name=f"band_attention_d{dil}",
    )(*([qkv] * len(in_specs)))


def _mix_kernel(x_ref, xp_ref, xn_ref, o0_ref, o1_ref, o2_ref, l0_ref, l1_ref, l2_ref,
                win_ref, ws_ref, bs_ref, cw_ref, pw_ref, ps_ref, wbr_ref, wg_ref, wo_ref,
                g1_ref, b1_ref, rwh_ref, rwl_ref, rb_ref,
                x1_ref, ti_ref, tp_ref,
                xe_ref, zb_ref, cb_ref, so1_ref, sl1_ref, so2_ref, sl2_ref, *, rows, tiles_per_seq, seq):
    tis = pl.program_id(0) % tiles_per_seq
    has_prev = tis > 0
    has_next = tis < tiles_per_seq - 1

    x = x_ref[...]
    xe_ref[0:rows, :] = x.astype(BF16)
    xe_ref[rows:rows + 2 * HALO, :] = jnp.concatenate([xp_ref[...], xn_ref[...]], axis=0).astype(BF16)
    xm = xe_ref[0:rows, :]

    h_a = _dot(xm, win_ref[:, 0:2 * A_WIDTH])
    a_u = h_a[:, 0:A_WIDTH]
    vn = _layer_norm(h_a[:, A_WIDTH:2 * A_WIDTH]).astype(BF16)
    gw = A_WIDTH // A_GROUPS
    chunks = []
    for c in range(rows // CHUNK):
        cols = [_dot(ws_ref[g], vn[c * CHUNK:(c + 1) * CHUNK, g * gw:(g + 1) * gw]) + bs_ref[g]
                for g in range(A_GROUPS)]
        chunks.append(jnp.concatenate(cols, axis=1))
    y_a = a_u * jnp.concatenate(chunks, axis=0)

    h_bc = _dot(xe_ref[...], win_ref[:, 2 * A_WIDTH:N_ABC])
    b_b = h_bc[0:rows, B_WIDTH:2 * B_WIDTH]
    z = h_bc[:, 2 * B_WIDTH:3 * B_WIDTH] * h_bc[:, 0:B_WIDTH]
    cz = h_bc[:, 3 * B_WIDTH:3 * B_WIDTH + C_WIDTH]
    for buf, val in ((zb_ref, z), (cb_ref, cz)):
        buf[0:HALO, :] = jnp.where(has_prev, val[rows:rows + HALO], 0.0)
        buf[HALO:HALO + rows, :] = val[0:rows]
        buf[HALO + rows:, :] = jnp.where(has_next, val[rows + HALO:], 0.0)

    conv = (cw_ref[0:1, :] * zb_ref[HALO - 1:HALO - 1 + rows, :]
            + cw_ref[1:2, :] * zb_ref[HALO:HALO + rows, :]
            + cw_ref[2:3, :] * zb_ref[HALO + 1:HALO + 1 + rows, :])
    y_b = b_b * conv

    pos = tis * rows + lax.broadcasted_iota(jnp.int32, (rows, 1), 0)
    cgw = C_WIDTH // len(POOL_WINDOWS)
    pooled_out = []
    for g, w in enumerate(POOL_WINDOWS):
        cs = slice(g * cgw, (g + 1) * cgw)
        acc = None
        for dd in range(-(w // 2), w // 2):
            v = cb_ref[HALO + dd:HALO + dd + rows, cs]
            acc = v if acc is None else acc + v
        count = (jnp.minimum(pos + w // 2, seq) - jnp.maximum(pos - w // 2, 0)).astype(F32)
        pooled = acc / count - cb_ref[HALO:HALO + rows, cs]
        pooled_out.append(_dot(pooled.astype(BF16), pw_ref[g]))
    y_c = jnp.concatenate(pooled_out, axis=1) * ps_ref[...]

    def token_order(o_ref, l_ref, so_ref, sl_ref):
        dil = o_ref.shape[0]
        if dil == 1:
            return [o_ref[0, :, h * HEAD_DIM:(h + 1) * HEAD_DIM].astype(F32) for h in range(ATT_HEADS)], l_ref[0]
        n = rows // dil
        for rho in range(dil):
            v = o_ref[rho].astype(F32)
            for h in range(ATT_HEADS):
                so_ref[h, pl.ds(rho, n, stride=dil), :] = v[:, h * HEAD_DIM:(h + 1) * HEAD_DIM]
            sl_ref[pl.ds(rho, n, stride=dil), :] = l_ref[rho]
        return [so_ref[h] for h in range(ATT_HEADS)], sl_ref[...]

    outs, lses = zip(token_order(o0_ref, l0_ref, None, None),
                     token_order(o1_ref, l1_ref, so1_ref, sl1_ref),
                     token_order(o2_ref, l2_ref, so2_ref, sl2_ref))
    top = jnp.maximum(jnp.maximum(lses[0], lses[1]), lses[2])
    es = [jnp.exp(l - top) for l in lses]
    den = es[0] + es[1] + es[2]
    wts = [e / den for e in es]
    heads = []
    for h in range(ATT_HEADS):
        acc = None
        for g in range(len(ATT_CONFIGS)):
            term = wts[g][:, h:h + 1] * outs[g][h]
            acc = term if acc is None else acc + term
        heads.append(acc)
    y_d = jnp.concatenate(heads, axis=1)

    merged = None
    for n, y in enumerate((y_a, y_b, y_c, y_d)):
        term = jax.nn.sigmoid(_dot(xm, wg_ref[n])) * _dot(y.astype(BF16), wbr_ref[n])
        merged = term if merged is None else merged + term
    u = ALPHA * x + _dot(merged.astype(BF16), wo_ref[...])
    x1 = _layer_norm(u) * g1_ref[...] + b1_ref[...]
    x1_ref[...] = x1.reshape(rows, SUBLANES, LANES)

    x1h = x1.astype(BF16)
    x1l = (x1 - x1h.astype(F32)).astype(BF16)
    logits = _dot(x1h, rwh_ref[...]) + _dot(x1l, rwh_ref[...]) + _dot(x1h, rwl_ref[...]) + rb_ref[...]
    lane = lax.broadcasted_iota(jnp.int32, (rows, LANES), 1)
    lane_f = lane.astype(F32)
    vals, idxs = [], []
    work = logits
    for _ in range(TOP_K):
        m = jnp.max(work, axis=-1, keepdims=True)
        idx = jnp.min(jnp.where(work == m, lane_f, float(LANES)), axis=-1, keepdims=True)
        vals.append(m)
        idxs.append(idx)
        work = jnp.where(lane_f == idx, -jnp.inf, work)
    exps = [jnp.exp(v - vals[0]) for v in vals]
    tot = exps[0] + exps[1] + exps[2] + exps[3]
    ti = jnp.zeros((rows, LANES), F32)
    tp = jnp.zeros((rows, LANES), F32)
    for k in range(TOP_K):
        ti = jnp.where(lane == k, idxs[k], ti)
        tp = jnp.where(lane == k, exps[k] / tot, tp)
    ti_ref[...] = ti.astype(jnp.int32)
    tp_ref[...] = tp


def _mixer(x2, seq, attn, lw):
    t, d = x2.shape
    assert d == SUBLANES * LANES
    rows = MIX_ROWS
    tps = seq // rows
    assert seq % rows == 0 and rows % CHUNK == 0
    halo_per_tile = rows // HALO
    last_halo = t // HALO - 1
    tile = lambda w: pl.BlockSpec((rows, w), lambda i: (i, 0))

    def sub_tile(arr):
        _, dil, _, w = arr.shape
        assert (rows // dil) % 16 == 0
        return pl.BlockSpec((None, dil, rows // dil, w), lambda i: (i // tps, 0, i % tps, 0))

    in_specs = [tile(d),
                pl.BlockSpec((HALO, d), lambda i: (jnp.maximum(i * halo_per_tile - 1, 0), 0)),
                pl.BlockSpec((HALO, d), lambda i: (jnp.minimum((i + 1) * halo_per_tile, last_halo), 0))]
    in_specs += [sub_tile(a[0]) for a in attn] + [sub_tile(a[1]) for a in attn]
    weights = (lw["w_abc"], lw["spatial_w"], lw["spatial_b"], lw["conv_w"], lw["pool_w"], lw["pool_scale"],
               lw["w_branch"], lw["w_gate"], lw["w_out"], lw["ln1_g"], lw["ln1_b"],
               lw["router_hi"], lw["router_lo"], lw["router_b"])
    in_specs += [_resident(w.shape) for w in weights]
    interleave = [pltpu.VMEM((ATT_HEADS, rows, HEAD_DIM), F32), pltpu.VMEM((rows, LANES), F32)]
    return pl.pallas_call(
        functools.partial(_mix_kernel, rows=rows, tiles_per_seq=tps, seq=seq),
        grid=(t // rows,),
        in_specs=in_specs,
        out_specs=[pl.BlockSpec((rows, SUBLANES, LANES), lambda i: (i, 0, 0)), tile(LANES), tile(LANES)],
        out_shape=[jax.ShapeDtypeStruct((t, SUBLANES, LANES), F32),
                   jax.ShapeDtypeStruct((t, LANES), jnp.int32),
                   jax.ShapeDtypeStruct((t, LANES), F32)],
        scratch_shapes=[pltpu.VMEM((rows + 2 * HALO, d), BF16),
                        pltpu.VMEM((rows + 2 * HALO, B_WIDTH), F32),
                        pltpu.VMEM((rows + 2 * HALO, C_WIDTH), F32)] + interleave + interleave,
        compiler_params=_params(1),
        name="mixer",
    )(x2, x2, x2, attn[0][0], attn[1][0], attn[2][0], attn[0][1], attn[1][1], attn[2][1], *weights)


def _gather_copy(x_hbm, idx_ref, gbuf, sem, i):
    return pltpu.make_async_copy(x_hbm.at[pl.ds(idx_ref[0, 0, i], 1)], gbuf.at[pl.ds(i, 1)], sem)


def _scatter_copy(obuf, idx_ref, y_hbm, sem, i):
    return pltpu.make_async_copy(obuf.at[pl.ds(i, 1)], y_hbm.at[pl.ds(idx_ref[0, 0, i], 1)], sem)


def _expert_kernel(be_ref, nu_ref, tok_ref, dst_ref, x_hbm, wgu_ref, bgu_ref, wd_ref, bd_ref, y_hbm,
                   gbuf, obuf, gsem, osem, *, n_blocks):
    m = EXPERT_ROWS
    s = pl.program_id(0)
    n_used = nu_ref[0]
    b = s - 1

    def scatter_wait(blk):
        slot = blk % 2
        pltpu.make_async_copy(obuf.at[slot], y_hbm.at[pl.ds(0, m)], osem.at[slot]).wait()

    @pl.when(s == 0)
    def _():
        obuf[0] = jnp.zeros(obuf.shape[1:], F32)
        n_rows = y_hbm.shape[0]
        for j in range(SCATTER_DEPTH):
            dump = pltpu.make_async_copy(obuf.at[0], y_hbm.at[pl.ds(n_rows - (j + 1) * m, m)], osem.at[0])
            dump.start()
            dump.wait()

    @pl.when((b >= 2) & (b - 2 < n_used))
    def _():
        scatter_wait(b - 2)

    @pl.when((s < n_blocks) & (s < n_used))
    def _():
        for i in range(m):
            _gather_copy(x_hbm, tok_ref, gbuf.at[s % 2], gsem.at[s % 2], i).start()

    @pl.when((b >= 0) & (b < n_used))
    def _():
        slot = b % 2
        pltpu.make_async_copy(x_hbm.at[pl.ds(0, m)], gbuf.at[slot], gsem.at[slot]).wait()
        xg = gbuf[slot].reshape(m, SUBLANES * LANES).astype(BF16)
        h = _dot(xg, wgu_ref[...]) + bgu_ref[...]
        g = jnp.minimum(h[:, :D_FF], SWIGLU_LIMIT)
        u = jnp.clip(h[:, D_FF:], -SWIGLU_LIMIT, SWIGLU_LIMIT)
        act = g * jax.nn.sigmoid(SWIGLU_ALPHA * g) * (u + 1.0)
        y = _dot(act.astype(BF16), wd_ref[...]) + bd_ref[...]
        obuf[slot] = y.reshape(m, SUBLANES, LANES)
        for i in range(m):
            _scatter_copy(obuf.at[slot], dst_ref, y_hbm, osem.at[slot], i).start()

    @pl.when(s == n_blocks)
    def _():
        @pl.when((b >= 1) & (b - 1 < n_used))
        def _():
            scatter_wait(b - 1)

        @pl.when(b < n_used)
        def _():
            scatter_wait(b)


def _experts(x1, slot_tok, slot_dst, block_e, n_used, lw):
    t = x1.shape[0]
    d = D_MODEL
    nb = block_e.shape[0]
    m = EXPERT_ROWS
    cur = lambda s: jnp.maximum(s - 1, 0)
    grid_spec = pltpu.PrefetchScalarGridSpec(
        num_scalar_prefetch=2,
        grid=(nb + 1,),
        in_specs=[pl.BlockSpec((1, 1, m), lambda s, be, nu: (jnp.minimum(s, nb - 1), 0, 0), memory_space=pltpu.SMEM),
                  pl.BlockSpec((1, 1, m), lambda s, be, nu: (cur(s), 0, 0), memory_space=pltpu.SMEM),
                  pl.BlockSpec(memory_space=pl.ANY),
                  pl.BlockSpec((None, d, 2 * D_FF), lambda s, be, nu: (be[cur(s)], 0, 0)),
                  pl.BlockSpec((None, 1, 2 * D_FF), lambda s, be, nu: (be[cur(s)], 0, 0)),
                  pl.BlockSpec((None, D_FF, d), lambda s, be, nu: (be[cur(s)], 0, 0)),
                  pl.BlockSpec((None, 1, d), lambda s, be, nu: (be[cur(s)], 0, 0))],
        out_specs=pl.BlockSpec(memory_space=pl.ANY),
        scratch_shapes=[pltpu.VMEM((2, m, SUBLANES, LANES), F32), pltpu.VMEM((2, m, SUBLANES, LANES), F32),
                        pltpu.SemaphoreType.DMA((2,)), pltpu.SemaphoreType.DMA((2,))],
    )
    return pl.pallas_call(
        functools.partial(_expert_kernel, n_blocks=nb),
        grid_spec=grid_spec,
        out_shape=jax.ShapeDtypeStruct((TOP_K * t + SCATTER_DEPTH * m, SUBLANES, LANES), F32),
        compiler_params=_params(1),
        name="expert_blocks",
    )(block_e, n_used, slot_tok, slot_dst, x1, lw["w_gate_up"], lw["b_gate_up"], lw["w_down"], lw["b_down"])


def _combine_kernel(y0_ref, y1_ref, y2_ref, y3_ref, x1_ref, p_ref, tp_ref, wpg_ref, wpp_ref, g2_ref, b2_ref, o_ref):
    rows = x1_ref.shape[0]
    x1 = x1_ref[...].reshape(rows, SUBLANES * LANES)
    tp = tp_ref[...]
    moe = None
    for k, y_ref in enumerate((y0_ref, y1_ref, y2_ref, y3_ref)):
        term = tp[:, k:k + 1] * y_ref[...].reshape(rows, SUBLANES * LANES)
        moe = term if moe is None else moe + term
    ple = jax.nn.sigmoid(_dot(x1.astype(BF16), wpg_ref[...])) * _dot(p_ref[...].astype(BF16), wpp_ref[...])
    o_ref[...] = _layer_norm(ALPHA * x1 + moe + ple) * g2_ref[...] + b2_ref[...]


def _combine(x1, p2, top_p, y, lw):
    t = x1.shape[0]
    d = D_MODEL
    rows = COMBINE_ROWS
    n_tiles = t // rows
    tile = lambda w: pl.BlockSpec((rows, w), lambda i: (i, 0))
    choice = lambda k: pl.BlockSpec((rows, SUBLANES, LANES), lambda i: (k * n_tiles + i, 0, 0))
    weights = (lw["w_ple_gate"], lw["w_ple_proj"], lw["ln2_g"], lw["ln2_b"])
    return pl.pallas_call(
        _combine_kernel,
        grid=(n_tiles,),
        in_specs=[choice(k) for k in range(TOP_K)] + [choice(0), tile(PLE_DIM), tile(LANES)]
        + [_resident(w.shape) for w in weights],
        out_specs=tile(d),
        out_shape=jax.ShapeDtypeStruct((t, d), F32),
        compiler_params=_params(1),
        name="combine",
    )(y, y, y, y, x1, p2, top_p, *weights)


def _routing_tables(top_i):
    t = top_i.shape[0]
    a = t * TOP_K
    m = EXPERT_ROWS
    nb = -(-(a + N_EXPERTS * (m - 1)) // m)
    e_flat = top_i.reshape(a)
    keys = jnp.sort(e_flat * a + jnp.arange(a, dtype=jnp.int32))
    a_sorted = keys % a
    experts = jnp.arange(N_EXPERTS, dtype=jnp.int32)
    start_sorted = jnp.sum(keys[None, :] < (experts * a)[:, None], axis=1, dtype=jnp.int32)
    counts = jnp.concatenate([start_sorted[1:], jnp.full((1,), a, jnp.int32)]) - start_sorted
    padded = (counts + m - 1) // m * m
    ends_padded = jnp.cumsum(padded)
    start_padded = ends_padded - padded
    block_start = jnp.arange(nb, dtype=jnp.int32) * m
    block_e = jnp.minimum(jnp.sum(block_start[:, None] >= ends_padded[None, :], axis=1, dtype=jnp.int32),
                          N_EXPERTS - 1)
    n_used = (ends_padded[-1] // m).astype(jnp.int32).reshape(1)
    within = jnp.arange(m, dtype=jnp.int32)[None, :]
    rank = block_start[:, None] + within - start_padded[block_e][:, None]
    valid = (rank >= 0) & (rank < counts[block_e][:, None])
    src = a_sorted[jnp.clip(start_sorted[block_e][:, None] + rank, 0, a - 1)]
    slot_tok = jnp.where(valid, src // TOP_K, 0)
    dump = a + (jnp.arange(nb, dtype=jnp.int32) % SCATTER_DEPTH)[:, None] * m + within
    slot_dst = jnp.where(valid, (src % TOP_K) * t + src // TOP_K, dump)
    return slot_tok.reshape(nb, 1, m), slot_dst.reshape(nb, 1, m), block_e, n_used


def _layer(x, p, lw):
    bn, s, d = x.shape
    x2 = x.reshape(bn * s, d)
    attn = []
    for g, (_, dil) in enumerate(ATT_CONFIGS):
        qkv = _qkv_proj(x, lw["w_qkv"][g], dil)
        attn.append(_attention(qkv, g))
    x1, top_i, top_p = _mixer(x2, s, attn, lw)
    slot_tok, slot_dst, block_e, n_used = _routing_tables(top_i[:, :TOP_K])
    y = _experts(x1, slot_tok, slot_dst, block_e, n_used, lw)
    out = _combine(x1, p.reshape(bn * s, PLE_DIM), top_p, y, lw)
    return out.reshape(bn, s, d)


def _prepare_layer(i, w_in, spatial_w, spatial_b, conv_w, pool_w, pool_scale, w_branch, w_gate, w_out, ln1_g, ln1_b,
                   router_w, router_b, w_gate_up, b_gate_up, w_down, b_down, w_ple_gate, w_ple_proj, ln2_g, ln2_b):
    rw = jnp.pad(router_w[i], ((0, 0), (0, LANES - N_EXPERTS)))
    rw_hi = rw.astype(BF16)
    row = lambda v: v[i].reshape(1, -1)
    return {
        "w_abc": w_in[i][:, :N_ABC].astype(BF16),
        "w_qkv": [w_in[i][:, N_ABC + g * N_QKV:N_ABC + (g + 1) * N_QKV].astype(BF16) for g in range(len(ATT_CONFIGS))],
        "spatial_w": spatial_w[i].astype(BF16),
        "spatial_b": jnp.broadcast_to(spatial_b[i][:, :, None], (A_GROUPS, CHUNK, A_WIDTH // A_GROUPS)),
        "conv_w": conv_w[i],
        "pool_w": pool_w[i].astype(BF16),
        "pool_scale": row(pool_scale),
        "w_branch": w_branch[i].astype(BF16),
        "w_gate": w_gate[i].astype(BF16),
        "w_out": w_out[i].astype(BF16),
        "ln1_g": row(ln1_g), "ln1_b": row(ln1_b),
        "router_hi": rw_hi,
        "router_lo": (rw - rw_hi.astype(F32)).astype(BF16),
        "router_b": jnp.pad(router_b[i], (0, LANES - N_EXPERTS), constant_values=NEG_INF).reshape(1, LANES),
        "w_gate_up": w_gate_up[i].astype(BF16),
        "b_gate_up": b_gate_up[i].reshape(N_EXPERTS, 1, 2 * D_FF),
        "w_down": w_down[i].astype(BF16),
        "b_down": b_down[i].reshape(N_EXPERTS, 1, D_MODEL),
        "w_ple_gate": w_ple_gate[i].astype(BF16),
        "w_ple_proj": w_ple_proj[i].astype(BF16),
        "ln2_g": row(ln2_g), "ln2_b": row(ln2_b),
    }


def _trunk(x, p, layers):
    for i, lw in enumerate(layers):
        x = _layer(x, p[i], lw)
    return x


def kernel(x_prompt, x_sample, p_prompt, p_sample, w_in, spatial_w, spatial_b, conv_w, pool_w, pool_scale, w_branch,
           w_gate, w_out, ln1_g, ln1_b, router_w, router_b, w_gate_up, b_gate_up, w_down, b_down, w_ple_gate,
           w_ple_proj, ln2_g, ln2_b):
    weights = (w_in, spatial_w, spatial_b, conv_w, pool_w, pool_scale, w_branch, w_gate, w_out, ln1_g, ln1_b,
               router_w, router_b, w_gate_up, b_gate_up, w_down, b_down, w_ple_gate, w_ple_proj, ln2_g, ln2_b)
    layers = [_prepare_layer(i, *weights) for i in range(w_in.shape[0])]
    return _trunk(x_prompt, p_prompt, layers), _trunk(x_sample, p_sample, layers)
```

```python
import functools

import numpy as np
import jax
import jax.numpy as jnp
from jax import lax
from jax.experimental import pallas as pl
from jax.experimental.pallas import tpu as pltpu

D_MODEL = 1024
PLE_DIM = 256
CHUNK = 128
A_WIDTH = 512
A_GROUPS = 4
B_WIDTH = 512
C_WIDTH = 512
POOL_WINDOWS = (2, 4, 8, 16)
ATT_CONFIGS = ((128, 1), (512, 4), (2048, 16))
ATT_HEADS = 4
HEAD_DIM = 128
D_WIDTH = ATT_HEADS * HEAD_DIM
N_ABC = 2 * A_WIDTH + 3 * B_WIDTH + C_WIDTH
N_QKV = 3 * D_WIDTH
N_EXPERTS = 32
TOP_K = 4
D_FF = 1024
SWIGLU_LIMIT = 7.0
SWIGLU_ALPHA = 1.702
DEPTH = 2
ALPHA = (2 * DEPTH) ** 0.25
LN_EPS = 1e-5
NEG_INF = -1e30

LANES = 128
SUBLANES = 8
HALO = 8
RADIUS = 64
Q_BLOCK = 128
PROJ_ROWS = 1024
ATT_ROWS = 512
MIX_ROWS = 512
EXPERT_ROWS = 256
COMBINE_ROWS = 512
SCATTER_DEPTH = 3
VMEM_LIMIT = 56 * 1024 * 1024

F32 = jnp.float32
BF16 = jnp.bfloat16


def _params(n_axes, vmem=VMEM_LIMIT):
    return pltpu.CompilerParams(dimension_semantics=("arbitrary",) * n_axes, vmem_limit_bytes=vmem)


def _resident(shape):
    zeros = (0,) * len(shape)
    return pl.BlockSpec(shape, lambda *_: zeros, pipeline_mode=pl.Buffered(1))


def _dot(a, b):
    return jnp.dot(a, b, preferred_element_type=F32)


def _layer_norm(x):
    mu = jnp.mean(x, axis=-1, keepdims=True)
    xc = x - mu
    var = jnp.mean(xc * xc, axis=-1, keepdims=True)
    return xc * lax.rsqrt(var + LN_EPS)


def _qkv_kernel(x_ref, w_ref, o_ref, *scratch, dil):
    rows, d = x_ref.shape
    if dil == 1:
        xe = x_ref[...].astype(BF16)
    else:
        lt_ref, xe_ref = scratch
        n = rows // dil
        for t in range(d // LANES):
            lt_ref[t] = x_ref[:, t * LANES:(t + 1) * LANES]
        for rho in range(dil):
            piece = jnp.concatenate([lt_ref[t, pl.ds(rho, n, stride=dil), :] for t in range(d // LANES)], axis=1)
            xe_ref[rho * n:(rho + 1) * n, :] = piece.astype(BF16)
        xe = xe_ref[...]
    y = _dot(xe, w_ref[...])
    o_ref[...] = y.astype(BF16).reshape(dil, rows // dil, N_QKV)


def _qkv_proj(x, w, dil):
    bn, s, d = x.shape
    rows = PROJ_ROWS
    assert s % rows == 0 and (rows // dil) % 16 == 0
    scratch = [] if dil == 1 else [pltpu.VMEM((d // LANES, rows, LANES), F32), pltpu.VMEM((rows, d), BF16)]
    return pl.pallas_call(
        functools.partial(_qkv_kernel, dil=dil),
        grid=(bn, s // rows),
        in_specs=[pl.BlockSpec((None, rows, d), lambda b, i: (b, i, 0)),
                  _resident((d, N_QKV))],
        out_specs=pl.BlockSpec((None, dil, rows // dil, N_QKV), lambda b, i: (b, 0, i, 0)),
        out_shape=jax.ShapeDtypeStruct((bn, dil, s // dil, N_QKV), BF16),
        scratch_shapes=scratch,
        compiler_params=_params(2),
        name=f"qkv_proj_d{dil}",
    )(x, w)


def _attn_kernel(*refs, sub, rows, dil, slopes, halo):
    if halo:
        main_ref, kp_ref, vp_ref, kn_ref, vn_ref, o_ref, lse_ref, kbuf, vbuf = refs
    else:
        main_ref, o_ref, lse_ref, kbuf, vbuf = refs
        for buf in (kbuf, vbuf):
            buf[0:RADIUS, :] = jnp.zeros((RADIUS, D_WIDTH), BF16)
            buf[RADIUS + rows:, :] = jnp.zeros((RADIUS, D_WIDTH), BF16)
    i = pl.program_id(2)

    window = Q_BLOCK + 2 * RADIUS
    row = lax.broadcasted_iota(jnp.int32, (Q_BLOCK, window), 0)
    col = lax.broadcasted_iota(jnp.int32, (Q_BLOCK, window), 1)
    dist = jnp.abs(col - RADIUS - row)
    in_band = dist <= RADIUS
    dist_f = (dist * dil).astype(F32)
    lane = lax.broadcasted_iota(jnp.int32, (Q_BLOCK, LANES), 1)
    scale = HEAD_DIM ** -0.5
    band_bias = [jnp.where(in_band, (-slopes[h]) * dist_f, NEG_INF) for h in range(ATT_HEADS)]
    for n in range(main_ref.shape[0]):
        if halo:
            kbuf[0:RADIUS, :] = kp_ref[...]
            kbuf[RADIUS + rows:, :] = kn_ref[...]
            vbuf[0:RADIUS, :] = vp_ref[...]
            vbuf[RADIUS + rows:, :] = vn_ref[...]
        kbuf[RADIUS:RADIUS + rows, :] = main_ref[n, :, D_WIDTH:2 * D_WIDTH]
        vbuf[RADIUS:RADIUS + rows, :] = main_ref[n, :, 2 * D_WIDTH:3 * D_WIDTH]
        for j in range(rows // Q_BLOCK):
            kpos = i * rows + j * Q_BLOCK - RADIUS + col
            in_seq = (kpos >= 0) & (kpos < sub)
            lse_tile = jnp.zeros((Q_BLOCK, LANES), F32)
            for h in range(ATT_HEADS):
                hs = slice(h * HEAD_DIM, (h + 1) * HEAD_DIM)
                q = main_ref[n, j * Q_BLOCK:(j + 1) * Q_BLOCK, hs]
                k = kbuf[j * Q_BLOCK:j * Q_BLOCK + window, hs]
                v = vbuf[j * Q_BLOCK:j * Q_BLOCK + window, hs]
                s = lax.dot_general(q, k, (((1,), (1,)), ((), ())), preferred_element_type=F32)
                s = jnp.where(in_seq, s * scale + band_bias[h], NEG_INF)
                m = jnp.max(s, axis=-1, keepdims=True)
                p = jnp.exp(s - m)
                l = jnp.sum(p, axis=-1, keepdims=True)
                o = _dot(p.astype(BF16), v) / l
                o_ref[n, j * Q_BLOCK:(j + 1) * Q_BLOCK, hs] = o.astype(BF16)
                lse_tile = jnp.where(lane == h, m + jnp.log(l), lse_tile)
            lse_ref[n, j * Q_BLOCK:(j + 1) * Q_BLOCK, :] = lse_tile


def _attention(qkv, group):
    bn, dil, sub, _ = qkv.shape
    rows = min(sub, ATT_ROWS)
    assert sub % rows == 0 and rows % Q_BLOCK == 0 and sub % RADIUS == 0
    n_groups = len(ATT_CONFIGS) * ATT_HEADS
    slopes = tuple(float(np.float32(2.0 ** (-8.0 * (group * ATT_HEADS + h + 1) / n_groups))) for h in range(ATT_HEADS))
    halo = sub > rows
    n_seq = 1 if halo else min(dil, ATT_ROWS // rows)
    assert dil % n_seq == 0
    halo_per_tile = rows // RADIUS
    last_halo = sub // RADIUS - 1

    def prev_map(col):
        return lambda b, r, i: (b, r, jnp.maximum(i * halo_per_tile - 1, 0), col)

    def next_map(col):
        return lambda b, r, i: (b, r, jnp.minimum((i + 1) * halo_per_tile, last_halo), col)

    tile = lambda w: pl.BlockSpec((None, n_seq, rows, w), lambda b, r, i: (b, r, i, 0))
    in_specs = [tile(N_QKV)]
    if halo:
        in_specs += [pl.BlockSpec((None, None, RADIUS, D_WIDTH), prev_map(1)),
                     pl.BlockSpec((None, None, RADIUS, D_WIDTH), prev_map(2)),
                     pl.BlockSpec((None, None, RADIUS, D_WIDTH), next_map(1)),
                     pl.BlockSpec((None, None, RADIUS, D_WIDTH), next_map(2))]
    return pl.pallas_call(
        functools.partial(_attn_kernel, sub=sub, rows=rows, dil=dil, slopes=slopes, halo=halo),
        grid=(bn, dil // n_seq, sub // rows),
        in_specs=in_specs,
        out_specs=[tile(D_WIDTH), tile(LANES)],
        out_shape=[jax.ShapeDtypeStruct((bn, dil, sub, D_WIDTH), BF16),
                   jax.ShapeDtypeStruct((bn, dil, sub, LANES), F32)],
        scratch_shapes=[pltpu.VMEM((rows + 2 * RADIUS, D_WIDTH), BF16),
                        pltpu.VMEM((rows + 2 * RADIUS, D_WIDTH), BF16)],
        compiler_params=_params(3),
        name=f"band_attention_d{dil}",
    )(*([qkv] * len(in_specs)))


def _mix_kernel(x_ref, xp_ref, xn_ref, o0_ref, o1_ref, o2_ref, l0_ref, l1_ref, l2_ref,
                win_ref, ws_ref, bs_ref, cw_ref, pw_ref, ps_ref, wbr_ref, wg_ref, wo_ref,
                g1_ref, b1_ref, rwc_ref, rb_ref,
                x1_ref, ti_ref, tp_ref,
                xe_ref, zb_ref, cb_ref, so1_ref, sl1_ref, so2_ref, sl2_ref, *, rows, tiles_per_seq, seq):
    tis = pl.program_id(0) % tiles_per_seq
    has_prev = tis > 0
    has_next = tis < tiles_per_seq - 1

    x = x_ref[...]
    xe_ref[0:rows, :] = x.astype(BF16)
    xe_ref[rows:rows + 2 * HALO, :] = jnp.concatenate([xp_ref[...], xn_ref[...]], axis=0).astype(BF16)
    xm = xe_ref[0:rows, :]

    h_a = _dot(xm, win_ref[:, 0:2 * A_WIDTH])
    a_u = h_a[:, 0:A_WIDTH]
    vn = _layer_norm(h_a[:, A_WIDTH:2 * A_WIDTH]).astype(BF16)
    gw = A_WIDTH // A_GROUPS
    chunks = []
    for c in range(rows // CHUNK):
        cols = [_dot(ws_ref[g], vn[c * CHUNK:(c + 1) * CHUNK, g * gw:(g + 1) * gw]) + bs_ref[g]
                for g in range(A_GROUPS)]
        chunks.append(jnp.concatenate(cols, axis=1))
    y_a = a_u * jnp.concatenate(chunks, axis=0)

    h_bc = _dot(xe_ref[...], win_ref[:, 2 * A_WIDTH:N_ABC])
    b_b = h_bc[0:rows, B_WIDTH:2 * B_WIDTH]
    z = h_bc[:, 2 * B_WIDTH:3 * B_WIDTH] * h_bc[:, 0:B_WIDTH]
    cz = h_bc[:, 3 * B_WIDTH:3 * B_WIDTH + C_WIDTH]
    for buf, val in ((zb_ref, z), (cb_ref, cz)):
        buf[0:HALO, :] = jnp.where(has_prev, val[rows:rows + HALO], 0.0)
        buf[HALO:HALO + rows, :] = val[0:rows]
        buf[HALO + rows:, :] = jnp.where(has_next, val[rows + HALO:], 0.0)

    conv = (cw_ref[0:1, :] * zb_ref[HALO - 1:HALO - 1 + rows, :]
            + cw_ref[1:2, :] * zb_ref[HALO:HALO + rows, :]
            + cw_ref[2:3, :] * zb_ref[HALO + 1:HALO + 1 + rows, :])
    y_b = b_b * conv

    pos = tis * rows + lax.broadcasted_iota(jnp.int32, (rows, 1), 0)
    cgw = C_WIDTH // len(POOL_WINDOWS)
    pooled_out = []
    for g, w in enumerate(POOL_WINDOWS):
        cs = slice(g * cgw, (g + 1) * cgw)
        acc = None
        for dd in range(-(w // 2), w // 2):
            v = cb_ref[HALO + dd:HALO + dd + rows, cs]
            acc = v if acc is None else acc + v
        count = (jnp.minimum(pos + w // 2, seq) - jnp.maximum(pos - w // 2, 0)).astype(F32)
        pooled = acc / count - cb_ref[HALO:HALO + rows, cs]
        pooled_out.append(_dot(pooled.astype(BF16), pw_ref[g]))
    y_c = jnp.concatenate(pooled_out, axis=1) * ps_ref[...]

    def token_order(o_ref, l_ref, so_ref, sl_ref):
        dil = o_ref.shape[0]
        if dil == 1:
            return [o_ref[0, :, h * HEAD_DIM:(h + 1) * HEAD_DIM].astype(F32) for h in range(ATT_HEADS)], l_ref[0]
        n = rows // dil
        for rho in range(dil):
            v = o_ref[rho].astype(F32)
            for h in range(ATT_HEADS):
                so_ref[h, pl.ds(rho, n, stride=dil), :] = v[:, h * HEAD_DIM:(h + 1) * HEAD_DIM]
            sl_ref[pl.ds(rho, n, stride=dil), :] = l_ref[rho]
        return [so_ref[h] for h in range(ATT_HEADS)], sl_ref[...]

    outs, lses = zip(token_order(o0_ref, l0_ref, None, None),
                     token_order(o1_ref, l1_ref, so1_ref, sl1_ref),
                     token_order(o2_ref, l2_ref, so2_ref, sl2_ref))
    top = jnp.maximum(jnp.maximum(lses[0], lses[1]), lses[2])
    es = [jnp.exp(l - top) for l in lses]
    den = es[0] + es[1] + es[2]
    wts = [e / den for e in es]
    heads = []
    for h in range(ATT_HEADS):
        acc = None
        for g in range(len(ATT_CONFIGS)):
            term = wts[g][:, h:h + 1] * outs[g][h]
            acc = term if acc is None else acc + term
        heads.append(acc)
    y_d = jnp.concatenate(heads, axis=1)

    merged = None
    for n, y in enumerate((y_a, y_b, y_c, y_d)):
        term = jax.nn.sigmoid(_dot(xm, wg_ref[n])) * _dot(y.astype(BF16), wbr_ref[n])
        merged = term if merged is None else merged + term
    u = ALPHA * x + _dot(merged.astype(BF16), wo_ref[...])
    x1 = _layer_norm(u) * g1_ref[...] + b1_ref[...]
    x1_ref[...] = x1.reshape(rows, SUBLANES, LANES)

    x1h = x1.astype(BF16)
    x1l = (x1 - x1h.astype(F32)).astype(BF16)
    hi_both = _dot(x1h, rwc_ref[...])
    logits = hi_both[:, :LANES] + _dot(x1l, rwc_ref[:, :LANES]) + hi_both[:, LANES:] + rb_ref[...]
    lane = lax.broadcasted_iota(jnp.int32, (rows, LANES), 1)
    lane_f = lane.astype(F32)
    vals, idxs = [], []
    work = logits
    for _ in range(TOP_K):
        m = jnp.max(work, axis=-1, keepdims=True)
        idx = jnp.min(jnp.where(work == m, lane_f, float(LANES)), axis=-1, keepdims=True)
        vals.append(m)
        idxs.append(idx)
        work = jnp.where(lane_f == idx, -jnp.inf, work)
    exps = [jnp.exp(v - vals[0]) for v in vals]
    tot = exps[0] + exps[1] + exps[2] + exps[3]
    ti = jnp.zeros((rows, LANES), F32)
    tp = jnp.zeros((rows, LANES), F32)
    for k in range(TOP_K):
        ti = jnp.where(lane == k, idxs[k], ti)
        tp = jnp.where(lane == k, exps[k] / tot, tp)
    ti_ref[...] = ti.astype(jnp.int32)
    tp_ref[...] = tp


def _mixer(x2, seq, attn, lw):
    t, d = x2.shape
    assert d == SUBLANES * LANES
    rows = MIX_ROWS
    tps = seq // rows
    assert seq % rows == 0 and rows % CHUNK == 0
    halo_per_tile = rows // HALO
    last_halo = t // HALO - 1
    tile = lambda w: pl.BlockSpec((rows, w), lambda i: (i, 0))

    def sub_tile(arr):
        _, dil, _, w = arr.shape
        assert (rows // dil) % 16 == 0
        return pl.BlockSpec((None, dil, rows // dil, w), lambda i: (i // tps, 0, i % tps, 0))

    in_specs = [tile(d),
                pl.BlockSpec((HALO, d), lambda i: (jnp.maximum(i * halo_per_tile - 1, 0), 0)),
                pl.BlockSpec((HALO, d), lambda i: (jnp.minimum((i + 1) * halo_per_tile, last_halo), 0))]
    in_specs += [sub_tile(a[0]) for a in attn] + [sub_tile(a[1]) for a in attn]
    weights = (lw["w_abc"], lw["spatial_w"], lw["spatial_b"], lw["conv_w"], lw["pool_w"], lw["pool_scale"],
               lw["w_branch"], lw["w_gate"], lw["w_out"], lw["ln1_g"], lw["ln1_b"],
               lw["router_w"], lw["router_b"])
    in_specs += [_resident(w.shape) for w in weights]
    interleave = [pltpu.VMEM((ATT_HEADS, rows, HEAD_DIM), F32), pltpu.VMEM((rows, LANES), F32)]
    return pl.pallas_call(
        functools.partial(_mix_kernel, rows=rows, tiles_per_seq=tps, seq=seq),
        grid=(t // rows,),
        in_specs=in_specs,
        out_specs=[pl.BlockSpec((rows, SUBLANES, LANES), lambda i: (i, 0, 0)), tile(LANES), tile(LANES)],
        out_shape=[jax.ShapeDtypeStruct((t, SUBLANES, LANES), F32),
                   jax.ShapeDtypeStruct((t, LANES), jnp.int32),
                   jax.ShapeDtypeStruct((t, LANES), F32)],
        scratch_shapes=[pltpu.VMEM((rows + 2 * HALO, d), BF16),
                        pltpu.VMEM((rows + 2 * HALO, B_WIDTH), F32),
                        pltpu.VMEM((rows + 2 * HALO, C_WIDTH), F32)] + interleave + interleave,
        compiler_params=_params(1),
        name="mixer",
    )(x2, x2, x2, attn[0][0], attn[1][0], attn[2][0], attn[0][1], attn[1][1], attn[2][1], *weights)


def _gather_copy(x_hbm, idx_ref, gbuf, sem, i):
    return pltpu.make_async_copy(x_hbm.at[pl.ds(idx_ref[0, 0, i], 1)], gbuf.at[pl.ds(i, 1)], sem)


def _scatter_copy(obuf, idx_ref, y_hbm, sem, i):
    return pltpu.make_async_copy(obuf.at[pl.ds(i, 1)], y_hbm.at[pl.ds(idx_ref[0, 0, i], 1)], sem)


def _expert_kernel(be_ref, nu_ref, tok_ref, dst_ref, x_hbm, wgu_ref, bgu_ref, wd_ref, bd_ref, y_hbm,
                   gbuf, obuf, gsem, osem, *, n_blocks):
    m = EXPERT_ROWS
    s = pl.program_id(0)
    n_used = nu_ref[0]
    b = s - 1

    def scatter_wait(blk):
        slot = blk % 2
        pltpu.make_async_copy(obuf.at[slot], y_hbm.at[pl.ds(0, m)], osem.at[slot]).wait()

    @pl.when(s == 0)
    def _():
        obuf[0] = jnp.zeros(obuf.shape[1:], F32)
        n_rows = y_hbm.shape[0]
        for j in range(SCATTER_DEPTH):
            dump = pltpu.make_async_copy(obuf.at[0], y_hbm.at[pl.ds(n_rows - (j + 1) * m, m)], osem.at[0])
            dump.start()
            dump.wait()

    @pl.when((b >= 2) & (b - 2 < n_used))
    def _():
        scatter_wait(b - 2)

    @pl.when((s < n_blocks) & (s < n_used))
    def _():
        for i in range(m):
            _gather_copy(x_hbm, tok_ref, gbuf.at[s % 2], gsem.at[s % 2], i).start()

    @pl.when((b >= 0) & (b < n_used))
    def _():
        slot = b % 2
        pltpu.make_async_copy(x_hbm.at[pl.ds(0, m)], gbuf.at[slot], gsem.at[slot]).wait()
        xg = gbuf[slot].reshape(m, SUBLANES * LANES).astype(BF16)
        h = _dot(xg, wgu_ref[...]) + bgu_ref[...]
        g = jnp.minimum(h[:, :D_FF], SWIGLU_LIMIT)
        u = jnp.clip(h[:, D_FF:], -SWIGLU_LIMIT, SWIGLU_LIMIT)
        act = g * jax.nn.sigmoid(SWIGLU_ALPHA * g) * (u + 1.0)
        y = _dot(act.astype(BF16), wd_ref[...]) + bd_ref[...]
        obuf[slot] = y.reshape(m, SUBLANES, LANES)
        for i in range(m):
            _scatter_copy(obuf.at[slot], dst_ref, y_hbm, osem.at[slot], i).start()

    @pl.when(s == n_blocks)
    def _():
        @pl.when((b >= 1) & (b - 1 < n_used))
        def _():
            scatter_wait(b - 1)

        @pl.when(b < n_used)
        def _():
            scatter_wait(b)


def _experts(x1, slot_tok, slot_dst, block_e, n_used, lw):
    t = x1.shape[0]
    d = D_MODEL
    nb = block_e.shape[0]
    m = EXPERT_ROWS
    cur = lambda s: jnp.maximum(s - 1, 0)
    grid_spec = pltpu.PrefetchScalarGridSpec(
        num_scalar_prefetch=2,
        grid=(nb + 1,),
        in_specs=[pl.BlockSpec((1, 1, m), lambda s, be, nu: (jnp.minimum(s, nb - 1), 0, 0), memory_space=pltpu.SMEM),
                  pl.BlockSpec((1, 1, m), lambda s, be, nu: (cur(s), 0, 0), memory_space=pltpu.SMEM),
                  pl.BlockSpec(memory_space=pl.ANY),
                  pl.BlockSpec((None, d, 2 * D_FF), lambda s, be, nu: (be[cur(s)], 0, 0)),
                  pl.BlockSpec((None, 1, 2 * D_FF), lambda s, be, nu: (be[cur(s)], 0, 0)),
                  pl.BlockSpec((None, D_FF, d), lambda s, be, nu: (be[cur(s)], 0, 0)),
                  pl.BlockSpec((None, 1, d), lambda s, be, nu: (be[cur(s)], 0, 0))],
        out_specs=pl.BlockSpec(memory_space=pl.ANY),
        scratch_shapes=[pltpu.VMEM((2, m, SUBLANES, LANES), F32), pltpu.VMEM((2, m, SUBLANES, LANES), F32),
                        pltpu.SemaphoreType.DMA((2,)), pltpu.SemaphoreType.DMA((2,))],
    )
    return pl.pallas_call(
        functools.partial(_expert_kernel, n_blocks=nb),
        grid_spec=grid_spec,
        out_shape=jax.ShapeDtypeStruct((TOP_K * t + SCATTER_DEPTH * m, SUBLANES, LANES), F32),
        compiler_params=_params(1),
        name="expert_blocks",
    )(block_e, n_used, slot_tok, slot_dst, x1, lw["w_gate_up"], lw["b_gate_up"], lw["w_down"], lw["b_down"])


def _combine_kernel(y0_ref, y1_ref, y2_ref, y3_ref, x1_ref, p_ref, tp_ref, wpg_ref, wpp_ref, g2_ref, b2_ref, o_ref):
    rows = x1_ref.shape[0]
    x1 = x1_ref[...].reshape(rows, SUBLANES * LANES)
    tp = tp_ref[...]
    moe = None
    for k, y_ref in enumerate((y0_ref, y1_ref, y2_ref, y3_ref)):
        term = tp[:, k:k + 1] * y_ref[...].reshape(rows, SUBLANES * LANES)
        moe = term if moe is None else moe + term
    ple = jax.nn.sigmoid(_dot(x1.astype(BF16), wpg_ref[...])) * _dot(p_ref[...].astype(BF16), wpp_ref[...])
    o_ref[...] = _layer_norm(ALPHA * x1 + moe + ple) * g2_ref[...] + b2_ref[...]


def _combine(x1, p2, top_p, y, lw):
    t = x1.shape[0]
    d = D_MODEL
    rows = COMBINE_ROWS
    n_tiles = t // rows
    tile = lambda w: pl.BlockSpec((rows, w), lambda i: (i, 0))
    choice = lambda k: pl.BlockSpec((rows, SUBLANES, LANES), lambda i: (k * n_tiles + i, 0, 0))
    weights = (lw["w_ple_gate"], lw["w_ple_proj"], lw["ln2_g"], lw["ln2_b"])
    return pl.pallas_call(
        _combine_kernel,
        grid=(n_tiles,),
        in_specs=[choice(k) for k in range(TOP_K)] + [choice(0), tile(PLE_DIM), tile(LANES)]
        + [_resident(w.shape) for w in weights],
        out_specs=tile(d),
        out_shape=jax.ShapeDtypeStruct((t, d), F32),
        compiler_params=_params(1),
        name="combine",
    )(y, y, y, y, x1, p2, top_p, *weights)


def _routing_tables(top_i):
    t = top_i.shape[0]
    a = t * TOP_K
    m = EXPERT_ROWS
    nb = -(-(a + N_EXPERTS * (m - 1)) // m)
    e_flat = top_i.reshape(a)
    keys = jnp.sort(e_flat * a + jnp.arange(a, dtype=jnp.int32))
    a_sorted = keys % a
    experts = jnp.arange(N_EXPERTS, dtype=jnp.int32)
    start_sorted = jnp.sum(keys[None, :] < (experts * a)[:, None], axis=1, dtype=jnp.int32)
    counts = jnp.concatenate([start_sorted[1:], jnp.full((1,), a, jnp.int32)]) - start_sorted
    padded = (counts + m - 1) // m * m
    ends_padded = jnp.cumsum(padded)
    start_padded = ends_padded - padded
    block_start = jnp.arange(nb, dtype=jnp.int32) * m
    block_e = jnp.minimum(jnp.sum(block_start[:, None] >= ends_padded[None, :], axis=1, dtype=jnp.int32),
                          N_EXPERTS - 1)
    n_used = (ends_padded[-1] // m).astype(jnp.int32).reshape(1)
    within = jnp.arange(m, dtype=jnp.int32)[None, :]
    rank = block_start[:, None] + within - start_padded[block_e][:, None]
    valid = (rank >= 0) & (rank < counts[block_e][:, None])
    src = a_sorted[jnp.clip(start_sorted[block_e][:, None] + rank, 0, a - 1)]
    slot_tok = jnp.where(valid, src // TOP_K, 0)
    dump = a + (jnp.arange(nb, dtype=jnp.int32) % SCATTER_DEPTH)[:, None] * m + within
    slot_dst = jnp.where(valid, (src % TOP_K) * t + src // TOP_K, dump)
    return slot_tok.reshape(nb, 1, m), slot_dst.reshape(nb, 1, m), block_e, n_used


def _layer(x, p, lw):
    bn, s, d = x.shape
    x2 = x.reshape(bn * s, d)
    attn = []
    for g, (_, dil) in enumerate(ATT_CONFIGS):
        qkv = _qkv_proj(x, lw["w_qkv"][g], dil)
        attn.append(_attention(qkv, g))
    x1, top_i, top_p = _mixer(x2, s, attn, lw)
    slot_tok, slot_dst, block_e, n_used = _routing_tables(top_i[:, :TOP_K])
    y = _experts(x1, slot_tok, slot_dst, block_e, n_used, lw)
    out = _combine(x1, p.reshape(bn * s, PLE_DIM), top_p, y, lw)
    return out.reshape(bn, s, d)


def _prepare_layer(i, w_in, spatial_w, spatial_b, conv_w, pool_w, pool_scale, w_branch, w_gate, w_out, ln1_g, ln1_b,
                   router_w, router_b, w_gate_up, b_gate_up, w_down, b_down, w_ple_gate, w_ple_proj, ln2_g, ln2_b):
    rw = jnp.pad(router_w[i], ((0, 0), (0, LANES - N_EXPERTS)))
    rw_hi = rw.astype(BF16)
    row = lambda v: v[i].reshape(1, -1)
    return {
        "w_abc": w_in[i][:, :N_ABC].astype(BF16),
        "w_qkv": [w_in[i][:, N_ABC + g * N_QKV:N_ABC + (g + 1) * N_QKV].astype(BF16) for g in range(len(ATT_CONFIGS))],
        "spatial_w": spatial_w[i].astype(BF16),
        "spatial_b": jnp.broadcast_to(spatial_b[i][:, :, None], (A_GROUPS, CHUNK, A_WIDTH // A_GROUPS)),
        "conv_w": conv_w[i],
        "pool_w": pool_w[i].astype(BF16),
        "pool_scale": row(pool_scale),
        "w_branch": w_branch[i].astype(BF16),
        "w_gate": w_gate[i].astype(BF16),
        "w_out": w_out[i].astype(BF16),
        "ln1_g": row(ln1_g), "ln1_b": row(ln1_b),
        "router_w": jnp.concatenate([rw_hi, (rw - rw_hi.astype(F32)).astype(BF16)], axis=1),
        "router_b": jnp.pad(router_b[i], (0, LANES - N_EXPERTS), constant_values=NEG_INF).reshape(1, LANES),
        "w_gate_up": w_gate_up[i].astype(BF16),
        "b_gate_up": b_gate_up[i].reshape(N_EXPERTS, 1, 2 * D_FF),
        "w_down": w_down[i].astype(BF16),
        "b_down": b_down[i].reshape(N_EXPERTS, 1, D_MODEL),
        "w_ple_gate": w_ple_gate[i].astype(BF16),
        "w_ple_proj": w_ple_proj[i].astype(BF16),
        "ln2_g": row(ln2_g), "ln2_b": row(ln2_b),
    }


def _trunk(x, p, layers):
    for i, lw in enumerate(layers):
        x = _layer(x, p[i], lw)
    return x


def kernel(x_prompt, x_sample, p_prompt, p_sample, w_in, spatial_w, spatial_b, conv_w, pool_w, pool_scale, w_branch,
           w_gate, w_out, ln1_g, ln1_b, router_w, router_b, w_gate_up, b_gate_up, w_down, b_down, w_ple_gate,
           w_ple_proj, ln2_g, ln2_b):
    weights = (w_in, spatial_w, spatial_b, conv_w, pool_w, pool_scale, w_branch, w_gate, w_out, ln1_g, ln1_b,
               router_w, router_b, w_gate_up, b_gate_up, w_down, b_down, w_ple_gate, w_ple_proj, ln2_g, ln2_b)
    layers = [_prepare_layer(i, *weights) for i in range(w_in.shape[0])]
    return _trunk(x_prompt, p_prompt, layers), _trunk(x_sample, p_sample, layers)
```

```python
import functools

import numpy as np
import jax
import jax.numpy as jnp
from jax import lax
from jax.experimental import pallas as pl
from jax.experimental.pallas import tpu as pltpu

D_MODEL = 1024
PLE_DIM = 256
CHUNK = 128
A_WIDTH = 512
A_GROUPS = 4
B_WIDTH = 512
C_WIDTH = 512
POOL_WINDOWS = (2, 4, 8, 16)
ATT_CONFIGS = ((128, 1), (512, 4), (2048, 16))
ATT_HEADS = 4
HEAD_DIM = 128
D_WIDTH = ATT_HEADS * HEAD_DIM
N_ABC = 2 * A_WIDTH + 3 * B_WIDTH + C_WIDTH
N_QKV = 3 * D_WIDTH
N_EXPERTS = 32
TOP_K = 4
D_FF = 1024
SWIGLU_LIMIT = 7.0
SWIGLU_ALPHA = 1.702
DEPTH = 2
ALPHA = (2 * DEPTH) ** 0.25
LN_EPS = 1e-5
NEG_INF = -1e30

LANES = 128
SUBLANES = 8
HALO = 8
RADIUS = 64
Q_BLOCK = 128
PROJ_ROWS = 1024
ATT_ROWS = 512
MIX_ROWS = 512
EXPERT_ROWS = 512
COMBINE_ROWS = 512
SCATTER_DEPTH = 3
VMEM_LIMIT = 56 * 1024 * 1024

F32 = jnp.float32
BF16 = jnp.bfloat16


def _params(n_axes, vmem=VMEM_LIMIT):
    return pltpu.CompilerParams(dimension_semantics=("arbitrary",) * n_axes, vmem_limit_bytes=vmem)


def _resident(shape):
    zeros = (0,) * len(shape)
    return pl.BlockSpec(shape, lambda *_: zeros, pipeline_mode=pl.Buffered(1))


def _dot(a, b):
    return jnp.dot(a, b, preferred_element_type=F32)


def _layer_norm(x):
    mu = jnp.mean(x, axis=-1, keepdims=True)
    xc = x - mu
    var = jnp.mean(xc * xc, axis=-1, keepdims=True)
    return xc * lax.rsqrt(var + LN_EPS)


def _qkv_kernel(x_ref, w_ref, o_ref, *scratch, dil):
    rows, d = x_ref.shape
    if dil == 1:
        xe = x_ref[...].astype(BF16)
    else:
        lt_ref, xe_ref = scratch
        n = rows // dil
        for t in range(d // LANES):
            lt_ref[t] = x_ref[:, t * LANES:(t + 1) * LANES]
        for rho in range(dil):
            piece = jnp.concatenate([lt_ref[t, pl.ds(rho, n, stride=dil), :] for t in range(d // LANES)], axis=1)
            xe_ref[rho * n:(rho + 1) * n, :] = piece.astype(BF16)
        xe = xe_ref[...]
    y = _dot(xe, w_ref[...])
    o_ref[...] = y.astype(BF16).reshape(dil, rows // dil, N_QKV)


def _qkv_proj(x, w, dil):
    bn, s, d = x.shape
    rows = PROJ_ROWS
    assert s % rows == 0 and (rows // dil) % 16 == 0
    scratch = [] if dil == 1 else [pltpu.VMEM((d // LANES, rows, LANES), F32), pltpu.VMEM((rows, d), BF16)]
    return pl.pallas_call(
        functools.partial(_qkv_kernel, dil=dil),
        grid=(bn, s // rows),
        in_specs=[pl.BlockSpec((None, rows, d), lambda b, i: (b, i, 0)),
                  _resident((d, N_QKV))],
        out_specs=pl.BlockSpec((None, dil, rows // dil, N_QKV), lambda b, i: (b, 0, i, 0)),
        out_shape=jax.ShapeDtypeStruct((bn, dil, s // dil, N_QKV), BF16),
        scratch_shapes=scratch,
        compiler_params=_params(2),
        name=f"qkv_proj_d{dil}",
    )(x, w)


def _attn_kernel(*refs, sub, rows, dil, slopes, halo):
    if halo:
        main_ref, kp_ref, vp_ref, kn_ref, vn_ref, o_ref, lse_ref, kbuf, vbuf = refs
    else:
        main_ref, o_ref, lse_ref, kbuf, vbuf = refs
        for buf in (kbuf, vbuf):
            buf[0:RADIUS, :] = jnp.zeros((RADIUS, D_WIDTH), BF16)
            buf[RADIUS + rows:, :] = jnp.zeros((RADIUS, D_WIDTH), BF16)
    i = pl.program_id(2)

    window = Q_BLOCK + 2 * RADIUS
    row = lax.broadcasted_iota(jnp.int32, (Q_BLOCK, window), 0)
    col = lax.broadcasted_iota(jnp.int32, (Q_BLOCK, window), 1)
    dist = jnp.abs(col - RADIUS - row)
    in_band = dist <= RADIUS
    dist_f = (dist * dil).astype(F32)
    lane = lax.broadcasted_iota(jnp.int32, (Q_BLOCK, LANES), 1)
    scale = HEAD_DIM ** -0.5
    band_bias = [jnp.where(in_band, (-slopes[h]) * dist_f, NEG_INF) for h in range(ATT_HEADS)]
    for n in range(main_ref.shape[0]):
        if halo:
            kbuf[0:RADIUS, :] = kp_ref[...]
            kbuf[RADIUS + rows:, :] = kn_ref[...]
            vbuf[0:RADIUS, :] = vp_ref[...]
            vbuf[RADIUS + rows:, :] = vn_ref[...]
        kbuf[RADIUS:RADIUS + rows, :] = main_ref[n, :, D_WIDTH:2 * D_WIDTH]
        vbuf[RADIUS:RADIUS + rows, :] = main_ref[n, :, 2 * D_WIDTH:3 * D_WIDTH]
        for j in range(rows // Q_BLOCK):
            kpos = i * rows + j * Q_BLOCK - RADIUS + col
            in_seq = (kpos >= 0) & (kpos < sub)
            lse_tile = jnp.zeros((Q_BLOCK, LANES), F32)
            for h in range(ATT_HEADS):
                hs = slice(h * HEAD_DIM, (h + 1) * HEAD_DIM)
                q = main_ref[n, j * Q_BLOCK:(j + 1) * Q_BLOCK, hs]
                k = kbuf[j * Q_BLOCK:j * Q_BLOCK + window, hs]
                v = vbuf[j * Q_BLOCK:j * Q_BLOCK + window, hs]
                s = lax.dot_general(q, k, (((1,), (1,)), ((), ())), preferred_element_type=F32)
                s = jnp.where(in_seq, s * scale + band_bias[h], NEG_INF)
                m = jnp.max(s, axis=-1, keepdims=True)
                p = jnp.exp(s - m)
                l = jnp.sum(p, axis=-1, keepdims=True)
                o = _dot(p.astype(BF16), v) / l
                o_ref[n, j * Q_BLOCK:(j + 1) * Q_BLOCK, hs] = o.astype(BF16)
                lse_tile = jnp.where(lane == h, m + jnp.log(l), lse_tile)
            lse_ref[n, j * Q_BLOCK:(j + 1) * Q_BLOCK, :] = lse_tile


def _attention(qkv, group):
    bn, dil, sub, _ = qkv.shape
    rows = min(sub, ATT_ROWS)
    assert sub % rows == 0 and rows % Q_BLOCK == 0 and sub % RADIUS == 0
    n_groups = len(ATT_CONFIGS) * ATT_HEADS
    slopes = tuple(float(np.float32(2.0 ** (-8.0 * (group * ATT_HEADS + h + 1) / n_groups))) for h in range(ATT_HEADS))
    halo = sub > rows
    n_seq = 1 if halo else min(dil, ATT_ROWS // rows)
    assert dil % n_seq == 0
    halo_per_tile = rows // RADIUS
    last_halo = sub // RADIUS - 1

    def prev_map(col):
        return lambda b, r, i: (b, r, jnp.maximum(i * halo_per_tile - 1, 0), col)

    def next_map(col):
        return lambda b, r, i: (b, r, jnp.minimum((i + 1) * halo_per_tile, last_halo), col)

    tile = lambda w: pl.BlockSpec((None, n_seq, rows, w), lambda b, r, i: (b, r, i, 0))
    in_specs = [tile(N_QKV)]
    if halo:
        in_specs += [pl.BlockSpec((None, None, RADIUS, D_WIDTH), prev_map(1)),
                     pl.BlockSpec((None, None, RADIUS, D_WIDTH), prev_map(2)),
                     pl.BlockSpec((None, None, RADIUS, D_WIDTH), next_map(1)),
                     pl.BlockSpec((None, None, RADIUS, D_WIDTH), next_map(2))]
    return pl.pallas_call(
        functools.partial(_attn_kernel, sub=sub, rows=rows, dil=dil, slopes=slopes, halo=halo),
        grid=(bn, dil // n_seq, sub // rows),
        in_specs=in_specs,
        out_specs=[tile(D_WIDTH), tile(LANES)],
        out_shape=[jax.ShapeDtypeStruct((bn, dil, sub, D_WIDTH), BF16),
                   jax.ShapeDtypeStruct((bn, dil, sub, LANES), F32)],
        scratch_shapes=[pltpu.VMEM((rows + 2 * RADIUS, D_WIDTH), BF16),
                        pltpu.VMEM((rows + 2 * RADIUS, D_WIDTH), BF16)],
        compiler_params=_params(3),
        name=f"band_attention_d{dil}",
    )(*([qkv] * len(in_specs)))


def _mix_kernel(x_ref, xp_ref, xn_ref, o0_ref, o1_ref, o2_ref, l0_ref, l1_ref, l2_ref,
                win_ref, ws_ref, bs_ref, cw_ref, pw_ref, ps_ref, wbr_ref, wg_ref, wo_ref,
                g1_ref, b1_ref, rwc_ref, rb_ref,
                x1_ref, ti_ref, tp_ref,
                xe_ref, zb_ref, cb_ref, so1_ref, sl1_ref, so2_ref, sl2_ref, *, rows, tiles_per_seq, seq):
    tis = pl.program_id(0) % tiles_per_seq
    has_prev = tis > 0
    has_next = tis < tiles_per_seq - 1

    x = x_ref[...]
    xe_ref[0:rows, :] = x.astype(BF16)
    xe_ref[rows:rows + 2 * HALO, :] = jnp.concatenate([xp_ref[...], xn_ref[...]], axis=0).astype(BF16)
    xm = xe_ref[0:rows, :]

    h_a = _dot(xm, win_ref[:, 0:2 * A_WIDTH])
    a_u = h_a[:, 0:A_WIDTH]
    vn = _layer_norm(h_a[:, A_WIDTH:2 * A_WIDTH]).astype(BF16)
    gw = A_WIDTH // A_GROUPS
    chunks = []
    for c in range(rows // CHUNK):
        cols = [_dot(ws_ref[g], vn[c * CHUNK:(c + 1) * CHUNK, g * gw:(g + 1) * gw]) + bs_ref[g]
                for g in range(A_GROUPS)]
        chunks.append(jnp.concatenate(cols, axis=1))
    y_a = a_u * jnp.concatenate(chunks, axis=0)

    h_bc = _dot(xe_ref[...], win_ref[:, 2 * A_WIDTH:N_ABC])
    b_b = h_bc[0:rows, B_WIDTH:2 * B_WIDTH]
    z = h_bc[:, 2 * B_WIDTH:3 * B_WIDTH] * h_bc[:, 0:B_WIDTH]
    cz = h_bc[:, 3 * B_WIDTH:3 * B_WIDTH + C_WIDTH]
    for buf, val in ((zb_ref, z), (cb_ref, cz)):
        buf[0:HALO, :] = jnp.where(has_prev, val[rows:rows + HALO], 0.0)
        buf[HALO:HALO + rows, :] = val[0:rows]
        buf[HALO + rows:, :] = jnp.where(has_next, val[rows + HALO:], 0.0)

    conv = (cw_ref[0:1, :] * zb_ref[HALO - 1:HALO - 1 + rows, :]
            + cw_ref[1:2, :] * zb_ref[HALO:HALO + rows, :]
            + cw_ref[2:3, :] * zb_ref[HALO + 1:HALO + 1 + rows, :])
    y_b = b_b * conv

    pos = tis * rows + lax.broadcasted_iota(jnp.int32, (rows, 1), 0)
    cgw = C_WIDTH // len(POOL_WINDOWS)
    pooled_out = []
    for g, w in enumerate(POOL_WINDOWS):
        cs = slice(g * cgw, (g + 1) * cgw)
        acc = None
        for dd in range(-(w // 2), w // 2):
            v = cb_ref[HALO + dd:HALO + dd + rows, cs]
            acc = v if acc is None else acc + v
        count = (jnp.minimum(pos + w // 2, seq) - jnp.maximum(pos - w // 2, 0)).astype(F32)
        pooled = acc / count - cb_ref[HALO:HALO + rows, cs]
        pooled_out.append(_dot(pooled.astype(BF16), pw_ref[g]))
    y_c = jnp.concatenate(pooled_out, axis=1) * ps_ref[...]

    def token_order(o_ref, l_ref, so_ref, sl_ref):
        dil = o_ref.shape[0]
        if dil == 1:
            return [o_ref[0, :, h * HEAD_DIM:(h + 1) * HEAD_DIM].astype(F32) for h in range(ATT_HEADS)], l_ref[0]
        n = rows // dil
        for rho in range(dil):
            v = o_ref[rho].astype(F32)
            for h in range(ATT_HEADS):
                so_ref[h, pl.ds(rho, n, stride=dil), :] = v[:, h * HEAD_DIM:(h + 1) * HEAD_DIM]
            sl_ref[pl.ds(rho, n, stride=dil), :] = l_ref[rho]
        return [so_ref[h] for h in range(ATT_HEADS)], sl_ref[...]

    outs, lses = zip(token_order(o0_ref, l0_ref, None, None),
                     token_order(o1_ref, l1_ref, so1_ref, sl1_ref),
                     token_order(o2_ref, l2_ref, so2_ref, sl2_ref))
    top = jnp.maximum(jnp.maximum(lses[0], lses[1]), lses[2])
    es = [jnp.exp(l - top) for l in lses]
    den = es[0] + es[1] + es[2]
    wts = [e / den for e in es]
    heads = []
    for h in range(ATT_HEADS):
        acc = None
        for g in range(len(ATT_CONFIGS)):
            term = wts[g][:, h:h + 1] * outs[g][h]
            acc = term if acc is None else acc + term
        heads.append(acc)
    y_d = jnp.concatenate(heads, axis=1)

    merged = None
    for n, y in enumerate((y_a, y_b, y_c, y_d)):
        term = jax.nn.sigmoid(_dot(xm, wg_ref[n])) * _dot(y.astype(BF16), wbr_ref[n])
        merged = term if merged is None else merged + term
    u = ALPHA * x + _dot(merged.astype(BF16), wo_ref[...])
    x1 = _layer_norm(u) * g1_ref[...] + b1_ref[...]
    x1_ref[...] = x1.reshape(rows, SUBLANES, LANES)

    x1h = x1.astype(BF16)
    x1l = (x1 - x1h.astype(F32)).astype(BF16)
    hi_both = _dot(x1h, rwc_ref[...])
    logits = hi_both[:, :LANES] + _dot(x1l, rwc_ref[:, :LANES]) + hi_both[:, LANES:] + rb_ref[...]
    lane = lax.broadcasted_iota(jnp.int32, (rows, LANES), 1)
    lane_f = lane.astype(F32)
    vals, idxs = [], []
    work = logits
    for _ in range(TOP_K):
        m = jnp.max(work, axis=-1, keepdims=True)
        idx = jnp.min(jnp.where(work == m, lane_f, float(LANES)), axis=-1, keepdims=True)
        vals.append(m)
        idxs.append(idx)
        work = jnp.where(lane_f == idx, -jnp.inf, work)
    exps = [jnp.exp(v - vals[0]) for v in vals]
    tot = exps[0] + exps[1] + exps[2] + exps[3]
    ti = jnp.zeros((rows, LANES), F32)
    tp = jnp.zeros((rows, LANES), F32)
    for k in range(TOP_K):
        ti = jnp.where(lane == k, idxs[k], ti)
        tp = jnp.where(lane == k, exps[k] / tot, tp)
    ti_ref[...] = ti.astype(jnp.int32)
    tp_ref[...] = tp


def _mixer(x2, seq, attn, lw):
    t, d = x2.shape
    assert d == SUBLANES * LANES
    rows = MIX_ROWS
    tps = seq // rows
    assert seq % rows == 0 and rows % CHUNK == 0
    halo_per_tile = rows // HALO
    last_halo = t // HALO - 1
    tile = lambda w: pl.BlockSpec((rows, w), lambda i: (i, 0))

    def sub_tile(arr):
        _, dil, _, w = arr.shape
        assert (rows // dil) % 16 == 0
        return pl.BlockSpec((None, dil, rows // dil, w), lambda i: (i // tps, 0, i % tps, 0))

    in_specs = [tile(d),
                pl.BlockSpec((HALO, d), lambda i: (jnp.maximum(i * halo_per_tile - 1, 0), 0)),
                pl.BlockSpec((HALO, d), lambda i: (jnp.minimum((i + 1) * halo_per_tile, last_halo), 0))]
    in_specs += [sub_tile(a[0]) for a in attn] + [sub_tile(a[1]) for a in attn]
    weights = (lw["w_abc"], lw["spatial_w"], lw["spatial_b"], lw["conv_w"], lw["pool_w"], lw["pool_scale"],
               lw["w_branch"], lw["w_gate"], lw["w_out"], lw["ln1_g"], lw["ln1_b"],
               lw["router_w"], lw["router_b"])
    in_specs += [_resident(w.shape) for w in weights]
    interleave = [pltpu.VMEM((ATT_HEADS, rows, HEAD_DIM), F32), pltpu.VMEM((rows, LANES), F32)]
    return pl.pallas_call(
        functools.partial(_mix_kernel, rows=rows, tiles_per_seq=tps, seq=seq),
        grid=(t // rows,),
        in_specs=in_specs,
        out_specs=[pl.BlockSpec((rows, SUBLANES, LANES), lambda i: (i, 0, 0)), tile(LANES), tile(LANES)],
        out_shape=[jax.ShapeDtypeStruct((t, SUBLANES, LANES), F32),
                   jax.ShapeDtypeStruct((t, LANES), jnp.int32),
                   jax.ShapeDtypeStruct((t, LANES), F32)],
        scratch_shapes=[pltpu.VMEM((rows + 2 * HALO, d), BF16),
                        pltpu.VMEM((rows + 2 * HALO, B_WIDTH), F32),
                        pltpu.VMEM((rows + 2 * HALO, C_WIDTH), F32)] + interleave + interleave,
        compiler_params=_params(1),
        name="mixer",
    )(x2, x2, x2, attn[0][0], attn[1][0], attn[2][0], attn[0][1], attn[1][1], attn[2][1], *weights)


def _gather_copy(x_hbm, idx_ref, gbuf, sem, i):
    return pltpu.make_async_copy(x_hbm.at[pl.ds(idx_ref[0, 0, i], 1)], gbuf.at[pl.ds(i, 1)], sem)


def _scatter_copy(obuf, idx_ref, y_hbm, sem, i):
    return pltpu.make_async_copy(obuf.at[pl.ds(i, 1)], y_hbm.at[pl.ds(idx_ref[0, 0, i], 1)], sem)


def _expert_kernel(be_ref, nu_ref, tok_ref, dst_ref, x_hbm, wgu_ref, bgu_ref, wd_ref, bd_ref, y_hbm,
                   gbuf, obuf, gsem, osem, *, n_blocks):
    m = EXPERT_ROWS
    s = pl.program_id(0)
    n_used = nu_ref[0]
    b = s - 1

    def scatter_wait(blk):
        slot = blk % 2
        pltpu.make_async_copy(obuf.at[slot], y_hbm.at[pl.ds(0, m)], osem.at[slot]).wait()

    @pl.when(s == 0)
    def _():
        obuf[0] = jnp.zeros(obuf.shape[1:], F32)
        n_rows = y_hbm.shape[0]
        for j in range(SCATTER_DEPTH):
            dump = pltpu.make_async_copy(obuf.at[0], y_hbm.at[pl.ds(n_rows - (j + 1) * m, m)], osem.at[0])
            dump.start()
            dump.wait()

    @pl.when((b >= 2) & (b - 2 < n_used))
    def _():
        scatter_wait(b - 2)

    @pl.when((s < n_blocks) & (s < n_used))
    def _():
        for i in range(m):
            _gather_copy(x_hbm, tok_ref, gbuf.at[s % 2], gsem.at[s % 2], i).start()

    @pl.when((b >= 0) & (b < n_used))
    def _():
        slot = b % 2
        pltpu.make_async_copy(x_hbm.at[pl.ds(0, m)], gbuf.at[slot], gsem.at[slot]).wait()
        xg = gbuf[slot].reshape(m, SUBLANES * LANES).astype(BF16)
        h = _dot(xg, wgu_ref[...]) + bgu_ref[...]
        g = jnp.minimum(h[:, :D_FF], SWIGLU_LIMIT)
        u = jnp.clip(h[:, D_FF:], -SWIGLU_LIMIT, SWIGLU_LIMIT)
        act = g * jax.nn.sigmoid(SWIGLU_ALPHA * g) * (u + 1.0)
        y = _dot(act.astype(BF16), wd_ref[...]) + bd_ref[...]
        obuf[slot] = y.reshape(m, SUBLANES, LANES)
        for i in range(m):
            _scatter_copy(obuf.at[slot], dst_ref, y_hbm, osem.at[slot], i).start()

    @pl.when(s == n_blocks)
    def _():
        @pl.when((b >= 1) & (b - 1 < n_used))
        def _():
            scatter_wait(b - 1)

        @pl.when(b < n_used)
        def _():
            scatter_wait(b)


def _experts(x1, slot_tok, slot_dst, block_e, n_used, lw):
    t = x1.shape[0]
    d = D_MODEL
    nb = block_e.shape[0]
    m = EXPERT_ROWS
    cur = lambda s: jnp.maximum(s - 1, 0)
    grid_spec = pltpu.PrefetchScalarGridSpec(
        num_scalar_prefetch=2,
        grid=(nb + 1,),
        in_specs=[pl.BlockSpec((1, 1, m), lambda s, be, nu: (jnp.minimum(s, nb - 1), 0, 0), memory_space=pltpu.SMEM),
                  pl.BlockSpec((1, 1, m), lambda s, be, nu: (cur(s), 0, 0), memory_space=pltpu.SMEM),
                  pl.BlockSpec(memory_space=pl.ANY),
                  pl.BlockSpec((None, d, 2 * D_FF), lambda s, be, nu: (be[cur(s)], 0, 0)),
                  pl.BlockSpec((None, 1, 2 * D_FF), lambda s, be, nu: (be[cur(s)], 0, 0)),
                  pl.BlockSpec((None, D_FF, d), lambda s, be, nu: (be[cur(s)], 0, 0)),
                  pl.BlockSpec((None, 1, d), lambda s, be, nu: (be[cur(s)], 0, 0))],
        out_specs=pl.BlockSpec(memory_space=pl.ANY),
        scratch_shapes=[pltpu.VMEM((2, m, SUBLANES, LANES), F32), pltpu.VMEM((2, m, SUBLANES, LANES), F32),
                        pltpu.SemaphoreType.DMA((2,)), pltpu.SemaphoreType.DMA((2,))],
    )
    return pl.pallas_call(
        functools.partial(_expert_kernel, n_blocks=nb),
        grid_spec=grid_spec,
        out_shape=jax.ShapeDtypeStruct((TOP_K * t + SCATTER_DEPTH * m, SUBLANES, LANES), F32),
        compiler_params=_params(1),
        name="expert_blocks",
    )(block_e, n_used, slot_tok, slot_dst, x1, lw["w_gate_up"], lw["b_gate_up"], lw["w_down"], lw["b_down"])


def _combine_kernel(y0_ref, y1_ref, y2_ref, y3_ref, x1_ref, p_ref, tp_ref, wpg_ref, wpp_ref, g2_ref, b2_ref, o_ref):
    rows = x1_ref.shape[0]
    x1 = x1_ref[...].reshape(rows, SUBLANES * LANES)
    tp = tp_ref[...]
    moe = None
    for k, y_ref in enumerate((y0_ref, y1_ref, y2_ref, y3_ref)):
        term = tp[:, k:k + 1] * y_ref[...].reshape(rows, SUBLANES * LANES)
        moe = term if moe is None else moe + term
    ple = jax.nn.sigmoid(_dot(x1.astype(BF16), wpg_ref[...])) * _dot(p_ref[...].astype(BF16), wpp_ref[...])
    o_ref[...] = _layer_norm(ALPHA * x1 + moe + ple) * g2_ref[...] + b2_ref[...]


def _combine(x1, p2, top_p, y, lw):
    t = x1.shape[0]
    d = D_MODEL
    rows = COMBINE_ROWS
    n_tiles = t // rows
    tile = lambda w: pl.BlockSpec((rows, w), lambda i: (i, 0))
    choice = lambda k: pl.BlockSpec((rows, SUBLANES, LANES), lambda i: (k * n_tiles + i, 0, 0))
    weights = (lw["w_ple_gate"], lw["w_ple_proj"], lw["ln2_g"], lw["ln2_b"])
    return pl.pallas_call(
        _combine_kernel,
        grid=(n_tiles,),
        in_specs=[choice(k) for k in range(TOP_K)] + [choice(0), tile(PLE_DIM), tile(LANES)]
        + [_resident(w.shape) for w in weights],
        out_specs=tile(d),
        out_shape=jax.ShapeDtypeStruct((t, d), F32),
        compiler_params=_params(1),
        name="combine",
    )(y, y, y, y, x1, p2, top_p, *weights)


def _routing_tables(top_i):
    t = top_i.shape[0]
    a = t * TOP_K
    m = EXPERT_ROWS
    nb = -(-(a + N_EXPERTS * (m - 1)) // m)
    e_flat = top_i.reshape(a)
    keys = jnp.sort(e_flat * a + jnp.arange(a, dtype=jnp.int32))
    a_sorted = keys % a
    experts = jnp.arange(N_EXPERTS, dtype=jnp.int32)
    start_sorted = jnp.sum(keys[None, :] < (experts * a)[:, None], axis=1, dtype=jnp.int32)
    counts = jnp.concatenate([start_sorted[1:], jnp.full((1,), a, jnp.int32)]) - start_sorted
    padded = (counts + m - 1) // m * m
    ends_padded = jnp.cumsum(padded)
    start_padded = ends_padded - padded
    block_start = jnp.arange(nb, dtype=jnp.int32) * m
    block_e = jnp.minimum(jnp.sum(block_start[:, None] >= ends_padded[None, :], axis=1, dtype=jnp.int32),
                          N_EXPERTS - 1)
    n_used = (ends_padded[-1] // m).astype(jnp.int32).reshape(1)
    within = jnp.arange(m, dtype=jnp.int32)[None, :]
    rank = block_start[:, None] + within - start_padded[block_e][:, None]
    valid = (rank >= 0) & (rank < counts[block_e][:, None])
    src = a_sorted[jnp.clip(start_sorted[block_e][:, None] + rank, 0, a - 1)]
    slot_tok = jnp.where(valid, src // TOP_K, 0)
    dump = a + (jnp.arange(nb, dtype=jnp.int32) % SCATTER_DEPTH)[:, None] * m + within
    slot_dst = jnp.where(valid, (src % TOP_K) * t + src // TOP_K, dump)
    return slot_tok.reshape(nb, 1, m), slot_dst.reshape(nb, 1, m), block_e, n_used


def _layer(x, p, lw):
    bn, s, d = x.shape
    x2 = x.reshape(bn * s, d)
    attn = []
    for g, (_, dil) in enumerate(ATT_CONFIGS):
        qkv = _qkv_proj(x, lw["w_qkv"][g], dil)
        attn.append(_attention(qkv, g))
    x1, top_i, top_p = _mixer(x2, s, attn, lw)
    slot_tok, slot_dst, block_e, n_used = _routing_tables(top_i[:, :TOP_K])
    y = _experts(x1, slot_tok, slot_dst, block_e, n_used, lw)
    out = _combine(x1, p.reshape(bn * s, PLE_DIM), top_p, y, lw)
    return out.reshape(bn, s, d)


def _prepare_layer(i, w_in, spatial_w, spatial_b, conv_w, pool_w, pool_scale, w_branch, w_gate, w_out, ln1_g, ln1_b,
                   router_w, router_b, w_gate_up, b_gate_up, w_down, b_down, w_ple_gate, w_ple_proj, ln2_g, ln2_b):
    rw = jnp.pad(router_w[i], ((0, 0), (0, LANES - N_EXPERTS)))
    rw_hi = rw.astype(BF16)
    row = lambda v: v[i].reshape(1, -1)
    return {
        "w_abc": w_in[i][:, :N_ABC].astype(BF16),
        "w_qkv": [w_in[i][:, N_ABC + g * N_QKV:N_ABC + (g + 1) * N_QKV].astype(BF16) for g in range(len(ATT_CONFIGS))],
        "spatial_w": spatial_w[i].astype(BF16),
        "spatial_b": jnp.broadcast_to(spatial_b[i][:, :, None], (A_GROUPS, CHUNK, A_WIDTH // A_GROUPS)),
        "conv_w": conv_w[i],
        "pool_w": pool_w[i].astype(BF16),
        "pool_scale": row(pool_scale),
        "w_branch": w_branch[i].astype(BF16),
        "w_gate": w_gate[i].astype(BF16),
        "w_out": w_out[i].astype(BF16),
        "ln1_g": row(ln1_g), "ln1_b": row(ln1_b),
        "router_w": jnp.concatenate([rw_hi, (rw - rw_hi.astype(F32)).astype(BF16)], axis=1),
        "router_b": jnp.pad(router_b[i], (0, LANES - N_EXPERTS), constant_values=NEG_INF).reshape(1, LANES),
        "w_gate_up": w_gate_up[i].astype(BF16),
        "b_gate_up": b_gate_up[i].reshape(N_EXPERTS, 1, 2 * D_FF),
        "w_down": w_down[i].astype(BF16),
        "b_down": b_down[i].reshape(N_EXPERTS, 1, D_MODEL),
        "w_ple_gate": w_ple_gate[i].astype(BF16),
        "w_ple_proj": w_ple_proj[i].astype(BF16),
        "ln2_g": row(ln2_g), "ln2_b": row(ln2_b),
    }


def _trunk(x, p, layers):
    for i, lw in enumerate(layers):
        x = _layer(x, p[i], lw)
    return x


def kernel(x_prompt, x_sample, p_prompt, p_sample, w_in, spatial_w, spatial_b, conv_w, pool_w, pool_scale, w_branch,
           w_gate, w_out, ln1_g, ln1_b, router_w, router_b, w_gate_up, b_gate_up, w_down, b_down, w_ple_gate,
           w_ple_proj, ln2_g, ln2_b):
    weights = (w_in, spatial_w, spatial_b, conv_w, pool_w, pool_scale, w_branch, w_gate, w_out, ln1_g, ln1_b,
               router_w, router_b, w_gate_up, b_gate_up, w_down, b_down, w_ple_gate, w_ple_proj, ln2_g, ln2_b)
    layers = [_prepare_layer(i, *weights) for i in range(w_in.shape[0])]
    return _trunk(x_prompt, p_prompt, layers), _trunk(x_sample, p_sample, layers)
```
